```python
import jax, jax.numpy as jnp
from jax import lax
import numpy as np

D_MODEL = 1024
BATCH = 1
SEQ = 16384
DEPTH = 2
DEC_BATCH = 8
DEC_SEQ = 64
PAST_LEN = 2048

CHUNK = 64
Q_BLOCK = 128
H_A = 4
DH_A = 64
D_A = H_A * DH_A
H_B = 4
DH_B = 64
D_B = H_B * DH_B
H_C = 4
DH_C = 128
D_C = H_C * DH_C
D_MIX = D_A + D_B + D_C
GDN_CONV = 4
D_FF = 2816
FFN_CONV = 3
IN_COLS = 3 * D_A + 4 * D_B + 2 * H_B + 4 * D_C + 2 * H_C
EPS = 1e-6
NEG = -1e30

kernel_name = 'hymba_sb_gdn_mlstm_stream_step'


def _rmsnorm(x, g):
    xf = x.astype(jnp.float32)
    y = xf * lax.rsqrt(jnp.mean(xf * xf, axis=-1, keepdims=True) + EPS)
    return (y * g.astype(jnp.float32)).astype(x.dtype)


def _l2norm(x):
    xf = x.astype(jnp.float32)
    return xf * lax.rsqrt(jnp.sum(xf * xf, axis=-1, keepdims=True) + EPS)


def _heads(x, h):
    b, t, _ = x.shape
    return x.reshape(b, t, h, -1).transpose(0, 2, 1, 3)


def _merge(x):
    b, h, t, d = x.shape
    return x.transpose(0, 2, 1, 3).reshape(b, t, h * d)


def _to_chunks(x, L):
    b, h, t = x.shape[:3]
    return jnp.moveaxis(x.reshape((b, h, t // L, L) + x.shape[3:]), 2, 0)


def _from_chunks(x):
    n, b, h, L = x.shape[:4]
    return jnp.moveaxis(x, 0, 2).reshape((b, h, n * L) + x.shape[4:])


def _causal_dwconv(x, hist, w):
    width, t = w.shape[0], x.shape[1]
    xp = jnp.concatenate([hist.astype(x.dtype), x], axis=1)
    y = xp[:, 0:t] * w[0]
    for j in range(1, width):
        y = y + xp[:, j:j + t] * w[j]
    return y, xp[:, t:]


def _split_cols(proj):
    sizes = [D_A, D_A, D_A, 3 * D_B, D_B, H_B, H_B, D_C, D_C, D_C, D_C, H_C, H_C]
    return jnp.split(proj, np.cumsum(sizes)[:-1].tolist(), axis=-1)


def _stick_breaking(q, k, v, q_start):
    b, h, t, d = q.shape
    tk = k.shape[2]
    qb = min(Q_BLOCK, t)
    nb = t // qb
    kf = k.astype(jnp.float32)
    vf = v.astype(jnp.float32)
    k_pos = jnp.arange(tk)
    qs = jnp.moveaxis((q.astype(jnp.float32) * d ** -0.5).reshape(b, h, nb, qb, d), 2, 0)

    def block(args):
        qblk, i = args
        q_pos = q_start + i * qb + jnp.arange(qb)
        z = jnp.einsum('bhqd,bhkd->bhqk', qblk, kf)
        before = k_pos[None, :] < q_pos[:, None]
        sp = jnp.where(before, jax.nn.softplus(z), 0.0)
        rest = lax.cumsum(sp, axis=3, reverse=True) - sp
        log_a = jnp.where(before, jax.nn.log_sigmoid(z) - rest, -jnp.inf)
        return jnp.einsum('bhqk,bhkd->bhqd', jnp.exp(log_a), vf)

    o = lax.map(block, (qs, jnp.arange(nb)))
    return jnp.moveaxis(o, 0, 2).reshape(b, h, t, d)


def _gdn_chunk(s, inp):
    q, k, v, beta, g = inp
    L = q.shape[2]
    dv = v.shape[-1]
    tri = jnp.tril(jnp.ones((L, L), bool))
    strict = jnp.tril(jnp.ones((L, L), bool), -1)
    gc = jnp.cumsum(g, axis=-1)
    gam = jnp.exp(jnp.where(tri, gc[..., :, None] - gc[..., None, :], -jnp.inf))
    m = jnp.where(strict, beta[..., :, None] * jnp.einsum('bhtd,bhsd->bhts', k, k) * gam, 0.0)
    rhs = jnp.concatenate([beta[..., None] * v, (beta * jnp.exp(gc))[..., None] * k], axis=-1)
    sol = lax.linalg.triangular_solve(m, rhs, left_side=True, lower=True, unit_diagonal=True)
    u, w = sol[..., :dv], sol[..., dv:]
    v_new = u - jnp.einsum('bhtk,bhkv->bhtv', w, s)
    qk = jnp.where(tri, jnp.einsum('bhtd,bhsd->bhts', q, k) * gam, 0.0)
    o = (jnp.einsum('bhtk,bhkv->bhtv', q * jnp.exp(gc)[..., None], s)
         + jnp.einsum('bhts,bhsv->bhtv', qk, v_new))
    gl = gc[..., -1:]
    s_new = (s * jnp.exp(gl)[..., None]
             + jnp.einsum('bhs,bhsk,bhsv->bhkv', jnp.exp(gl - gc), k, v_new))
    return s_new, o


def _mlstm_chunk(carry, inp):
    c, n, m0 = carry
    q, k, v, ig, lf = inp
    L = q.shape[2]
    tri = jnp.tril(jnp.ones((L, L), bool))
    bcum = jnp.cumsum(lf, axis=-1)
    dmat = jnp.where(tri, bcum[..., :, None] - bcum[..., None, :] + ig[..., None, :], -jnp.inf)
    g = bcum + m0[..., None]
    m = jnp.maximum(g, jnp.max(dmat, axis=-1))
    w = jnp.exp(dmat - m[..., None])
    inter = jnp.exp(g - m)
    qk = jnp.einsum('bhtd,bhsd->bhts', q, k) * w
    num = (inter[..., None] * jnp.einsum('bhtk,bhkv->bhtv', q, c)
           + jnp.einsum('bhts,bhsv->bhtv', qk, v))
    den = inter * jnp.einsum('bhtk,bhk->bht', q, n) + jnp.sum(qk, axis=-1)
    h = num / jnp.maximum(jnp.abs(den), jnp.exp(-m))[..., None]
    m_last = m[..., -1]
    decay = jnp.exp(g[..., -1] - m_last)
    wk = jnp.exp(bcum[..., -1:] - bcum + ig - m_last[..., None])
    c_new = decay[..., None, None] * c + jnp.einsum('bhs,bhsk,bhsv->bhkv', wk, k, v)
    n_new = decay[..., None] * n + jnp.einsum('bhs,bhsk->bhk', wk, k)
    return (c_new, n_new, m_last), h


def _layer(x, kv_k, kv_v, gdn_hist, gdn_s, m_c, m_n, m_m, ffn_hist,
           g_mix_pre, g_mix_post, g_ffn_pre, g_ffn_post, w_in, gdn_conv_w, gdn_a_log,
           gdn_dt_bias, gdn_norm_g, mlstm_b_i, mlstm_b_f, mlstm_norm_g, w_out,
           ffn_w_up, ffn_conv_w, ffn_w_down):
    dt = x.dtype
    f32 = jnp.float32
    t = x.shape[1]
    L = min(CHUNK, t)
    h = _rmsnorm(x, g_mix_pre)
    (q_a, k_a, v_a, qkv_b, z_b, beta_b, a_b,
     q_c, k_c, v_c, o_c, i_c, f_c) = _split_cols(h @ w_in)

    q_a, k_a, v_a = _heads(q_a, H_A), _heads(k_a, H_A), _heads(v_a, H_A)
    k_all = jnp.concatenate([kv_k.astype(dt), k_a], axis=2)
    v_all = jnp.concatenate([kv_v.astype(dt), v_a], axis=2)
    out_a = _merge(_stick_breaking(q_a, k_all, v_all, kv_k.shape[2]).astype(dt))

    conv_b, gdn_hist_new = _causal_dwconv(qkv_b, gdn_hist, gdn_conv_w)
    q_b, k_b, v_b = jnp.split(jax.nn.silu(conv_b), 3, axis=-1)
    q_b = _l2norm(_heads(q_b, H_B)) * DH_B ** -0.5
    k_b = _l2norm(_heads(k_b, H_B))
    v_b = _heads(v_b, H_B).astype(f32)
    beta = jax.nn.sigmoid(beta_b.astype(f32)).transpose(0, 2, 1)
    g_log = (-jnp.exp(gdn_a_log.astype(f32))
             * jax.nn.softplus(a_b.astype(f32) + gdn_dt_bias.astype(f32))).transpose(0, 2, 1)
    xs_b = (_to_chunks(q_b, L), _to_chunks(k_b, L), _to_chunks(v_b, L),
            _to_chunks(beta, L), _to_chunks(g_log, L))
    s_new, o_b = lax.scan(_gdn_chunk, gdn_s.astype(f32), xs_b)
    o_b = _rmsnorm(_from_chunks(o_b), gdn_norm_g) * jax.nn.silu(_heads(z_b, H_B).astype(f32))
    out_b = _merge(o_b.astype(dt))

    q_c = _heads(q_c, H_C).astype(f32)
    k_c = _heads(k_c, H_C).astype(f32) * DH_C ** -0.5
    v_c = _heads(v_c, H_C).astype(f32)
    ig = (i_c.astype(f32) + mlstm_b_i.astype(f32)).transpose(0, 2, 1)
    lf = jax.nn.log_sigmoid(f_c.astype(f32) + mlstm_b_f.astype(f32)).transpose(0, 2, 1)
    xs_c = (_to_chunks(q_c, L), _to_chunks(k_c, L), _to_chunks(v_c, L),
            _to_chunks(ig, L), _to_chunks(lf, L))
    (c_new, n_new, m_new), h_c = lax.scan(
        _mlstm_chunk, (m_c.astype(f32), m_n.astype(f32), m_m.astype(f32)), xs_c)
    h_c = _rmsnorm(_from_chunks(h_c), mlstm_norm_g) * jax.nn.sigmoid(_heads(o_c, H_C).astype(f32))
    out_c = _merge(h_c.astype(dt))

    mix = jnp.concatenate([out_a, out_b, out_c], axis=-1) @ w_out
    x = x + _rmsnorm(mix, g_mix_post)

    h = _rmsnorm(x, g_ffn_pre)
    gate, up = jnp.split(h @ ffn_w_up, 2, axis=-1)
    gate, ffn_hist_new = _causal_dwconv(gate, ffn_hist, ffn_conv_w)
    y = (jax.nn.gelu(gate, approximate=True) * up) @ ffn_w_down
    x = x + _rmsnorm(y, g_ffn_post)
    return (x, k_a, v_a, gdn_hist_new.astype(dt), s_new.astype(dt), c_new.astype(dt),
            n_new.astype(dt), m_new.astype(dt), ffn_hist_new.astype(dt))


def setup_inputs(seed: int = 0) -> dict:
    key = jax.random.key(seed)
    ks = jax.random.split(key, 32)

    def nrm(k, shape, s=1.0):
        return s * jax.random.normal(k, shape, jnp.float32)

    dt_init = jnp.exp(jax.random.uniform(ks[20], (DEPTH, H_B), jnp.float32,
                                         np.log(1e-3), np.log(1e-1)))
    return {
        'x_prompt': nrm(ks[0], (BATCH, SEQ, D_MODEL)),
        'x_sample': nrm(ks[1], (DEC_BATCH, DEC_SEQ, D_MODEL)),
        'cache_sb_k': nrm(ks[2], (DEPTH, DEC_BATCH, H_A, PAST_LEN, DH_A)),
        'cache_sb_v': nrm(ks[3], (DEPTH, DEC_BATCH, H_A, PAST_LEN, DH_A)),
        'state_gdn_conv': nrm(ks[4], (DEPTH, DEC_BATCH, GDN_CONV - 1, 3 * D_B)),
        'state_gdn_s': nrm(ks[5], (DEPTH, DEC_BATCH, H_B, DH_B, DH_B), 0.1),
        'state_mlstm_c': nrm(ks[6], (DEPTH, DEC_BATCH, H_C, DH_C, DH_C), 0.1),
        'state_mlstm_n': nrm(ks[7], (DEPTH, DEC_BATCH, H_C, DH_C), 0.1),
        'state_mlstm_m': nrm(ks[8], (DEPTH, DEC_BATCH, H_C)),
        'state_ffn_conv': nrm(ks[9], (DEPTH, DEC_BATCH, FFN_CONV - 1, D_FF)),
        'g_mix_pre': 1.0 + nrm(ks[10], (DEPTH, D_MODEL), 0.02),
        'g_mix_post': 1.0 + nrm(ks[11], (DEPTH, D_MODEL), 0.02),
        'g_ffn_pre': 1.0 + nrm(ks[12], (DEPTH, D_MODEL), 0.02),
        'g_ffn_post': 1.0 + nrm(ks[13], (DEPTH, D_MODEL), 0.02),
        'w_in': nrm(ks[14], (DEPTH, D_MODEL, IN_COLS), D_MODEL ** -0.5),
        'gdn_conv_w': nrm(ks[15], (DEPTH, GDN_CONV, 3 * D_B), GDN_CONV ** -0.5),
        'gdn_a_log': jnp.log(jax.random.uniform(ks[16], (DEPTH, H_B), jnp.float32, 1.0, 16.0)),
        'gdn_dt_bias': dt_init + jnp.log(-jnp.expm1(-dt_init)),
        'gdn_norm_g': 1.0 + nrm(ks[17], (DEPTH, DH_B), 0.02),
        'mlstm_b_i': nrm(ks[18], (DEPTH, H_C), 0.1),
        'mlstm_b_f': jnp.linspace(3.0, 6.0, H_C)[None, :] + nrm(ks[19], (DEPTH, H_C), 0.1),
        'mlstm_norm_g': 1.0 + nrm(ks[21], (DEPTH, DH_C), 0.02),
        'w_out': nrm(ks[22], (DEPTH, D_MIX, D_MODEL), D_MIX ** -0.5),
        'ffn_w_up': nrm(ks[23], (DEPTH, D_MODEL, 2 * D_FF), D_MODEL ** -0.5),
        'ffn_conv_w': nrm(ks[24], (DEPTH, FFN_CONV, D_FF), FFN_CONV ** -0.5),
        'ffn_w_down': nrm(ks[25], (DEPTH, D_FF, D_MODEL), D_FF ** -0.5),
    }


def reference(x_prompt, x_sample, cache_sb_k, cache_sb_v, state_gdn_conv, state_gdn_s,
              state_mlstm_c, state_mlstm_n, state_mlstm_m, state_ffn_conv,
              g_mix_pre, g_mix_post, g_ffn_pre, g_ffn_post, w_in, gdn_conv_w, gdn_a_log,
              gdn_dt_bias, gdn_norm_g, mlstm_b_i, mlstm_b_f, mlstm_norm_g, w_out,
              ffn_w_up, ffn_conv_w, ffn_w_down):
    pdt = x_prompt.dtype
    b = x_prompt.shape[0]
    f32 = jnp.float32
    empty_kv = jnp.zeros((b, H_A, 0, DH_A), pdt)
    zero_gdn_hist = jnp.zeros((b, GDN_CONV - 1, 3 * D_B), pdt)
    zero_s = jnp.zeros((b, H_B, DH_B, DH_B), f32)
    zero_c = jnp.zeros((b, H_C, DH_C, DH_C), f32)
    zero_n = jnp.zeros((b, H_C, DH_C), f32)
    m_init = jnp.full((b, H_C), NEG, f32)
    zero_ffn_hist = jnp.zeros((b, FFN_CONV - 1, D_FF), pdt)

    xp, xs = x_prompt, x_sample
    new_p, new_s = [], []
    for l in range(DEPTH):
        lw = (g_mix_pre[l], g_mix_post[l], g_ffn_pre[l], g_ffn_post[l], w_in[l], gdn_conv_w[l],
              gdn_a_log[l], gdn_dt_bias[l], gdn_norm_g[l], mlstm_b_i[l], mlstm_b_f[l],
              mlstm_norm_g[l], w_out[l], ffn_w_up[l], ffn_conv_w[l], ffn_w_down[l])
        xp, *st_p = _layer(xp, empty_kv, empty_kv, zero_gdn_hist, zero_s, zero_c, zero_n,
                           m_init, zero_ffn_hist, *lw)
        xs, *st_s = _layer(xs, cache_sb_k[l], cache_sb_v[l], state_gdn_conv[l], state_gdn_s[l],
                           state_mlstm_c[l], state_mlstm_n[l], state_mlstm_m[l],
                           state_ffn_conv[l], *lw)
        new_p.append(st_p)
        new_s.append(st_s)
    (sb_k_p, sb_v_p, gdn_conv_p, gdn_s_p, mlstm_c_p, mlstm_n_p, mlstm_m_p,
     ffn_conv_p) = [jnp.stack(a) for a in zip(*new_p)]
    (sb_k_s, sb_v_s, gdn_conv_s, gdn_s_s, mlstm_c_s, mlstm_n_s, mlstm_m_s,
     ffn_conv_s) = [jnp.stack(a) for a in zip(*new_s)]
    return (xp, xs,
            sb_k_p, sb_v_p, gdn_conv_p, gdn_s_p, mlstm_c_p, mlstm_n_p, mlstm_m_p, ffn_conv_p,
            sb_k_s, sb_v_s, gdn_conv_s, gdn_s_s, mlstm_c_s, mlstm_n_s, mlstm_m_s, ffn_conv_s)
```

```python
import functools

import jax
import jax.numpy as jnp
import numpy as np
from jax import lax
from jax.experimental import pallas as pl
from jax.experimental.pallas import tpu as pltpu

F32 = jnp.float32
BF16 = jnp.bfloat16

D_MODEL = 1024
CHUNK = 64
H_A, DH_A = 4, 64
H_B, DH_B = 4, 64
H_C, DH_C = 4, 128
D_A, D_B, D_C = H_A * DH_A, H_B * DH_B, H_C * DH_C
GDN_CONV = 4
D_FF = 2816
FFN_CONV = 3
EPS = 1e-6
NEG = -1e30

LANES = 128
SUBLANES = 8
ROW_TILE = 512
ATTN_TQ = 256
ATTN_TK = 256
FF_CHUNK = 256
VMEM_LIMIT = 56 * 1024 * 1024

G_BETA, G_A, G_I, G_F = 0, 4, 8, 12


def _params(sem):
    return pltpu.CompilerParams(dimension_semantics=sem, vmem_limit_bytes=VMEM_LIMIT)


def _dot(a, b):
    return jnp.dot(a.astype(BF16), b.astype(BF16), preferred_element_type=F32)


def _dot_nt(a, b):
    return lax.dot_general(a.astype(BF16), b.astype(BF16), (((1,), (1,)), ((), ())),
                           preferred_element_type=F32)


def _split(a):
    hi = a.astype(BF16)
    lo = (a - hi.astype(F32)).astype(BF16)
    return hi, lo


def _dot3(a, b):
    ah, al = _split(a)
    bh, bl = _split(b)
    return (jnp.dot(ah, bh, preferred_element_type=F32)
            + jnp.dot(ah, bl, preferred_element_type=F32)
            + jnp.dot(al, bh, preferred_element_type=F32))


def _dot_exact_lhs(a01, b):
    a = a01.astype(BF16)
    b0 = b.astype(BF16)
    r1 = b - b0.astype(F32)
    b1 = r1.astype(BF16)
    b2 = (r1 - b1.astype(F32)).astype(BF16)
    return (jnp.dot(a, b0, preferred_element_type=F32)
            + jnp.dot(a, b1, preferred_element_type=F32)
            + jnp.dot(a, b2, preferred_element_type=F32))


def _sigmoid(x):
    return 1.0 / (1.0 + jnp.exp(-x))


def _softplus(x):
    return jnp.maximum(x, 0.0) + jnp.log(1.0 + jnp.exp(-jnp.abs(x)))


def _rms(x, g):
    return x * lax.rsqrt(jnp.mean(x * x, axis=-1, keepdims=True) + EPS) * g


def _inproj_kernel(x_ref, g_ref, w_ref, wg_ref, a_ref, b_ref, z_ref, c_ref, gt_ref):
    h = _rms(x_ref[...], g_ref[...]).astype(BF16)
    a_ref[...] = jnp.dot(h, w_ref[:, 0:768], preferred_element_type=F32)
    b_ref[...] = jnp.dot(h, w_ref[:, 768:1536], preferred_element_type=F32)
    z_ref[...] = jnp.dot(h, w_ref[:, 1536:1792], preferred_element_type=F32)
    c_ref[...] = jnp.dot(h, w_ref[:, 1792:3840], preferred_element_type=F32)
    gt_ref[...] = jnp.dot(h, wg_ref[...], preferred_element_type=F32)


def _inproj(x, g, w_main, w_gate):
    n = x.shape[0]
    r = min(ROW_TILE, n)
    widths = (768, 768, 256, 2048, LANES)
    row = lambda i: (i, 0)
    fixed = lambda i: (0, 0)
    return pl.pallas_call(
        _inproj_kernel,
        grid=(n // r,),
        in_specs=[pl.BlockSpec((r, D_MODEL), row),
                  pl.BlockSpec((1, D_MODEL), fixed),
                  pl.BlockSpec(w_main.shape, fixed),
                  pl.BlockSpec(w_gate.shape, fixed)],
        out_specs=[pl.BlockSpec((r, w), row) for w in widths],
        out_shape=[jax.ShapeDtypeStruct((n, w), F32) for w in widths],
        compiler_params=_params(("parallel",)),
        name="inproj",
    )(x, g, w_main, w_gate)


def _attn_kernel(q_ref, k_ref, v_ref, u_ref, o_ref, acc_scr, car_scr, *, tq, tk, q_start):
    i = pl.program_id(1)
    q = q_ref[0] * jnp.asarray(DH_A ** -0.5, BF16)
    q_pos0 = q_start + i * tq
    n_all = (q_pos0 + tq - 1 + tk - 1) // tk
    n_full = q_pos0 // tk
    acc_scr[...] = jnp.zeros_like(acc_scr)
    car_scr[...] = jnp.zeros_like(car_scr)

    def block(j, masked):
        ks = pl.multiple_of(j * tk, tk)
        kb = k_ref[0, pl.ds(ks, tk), :]
        vb = v_ref[0, pl.ds(ks, tk), :]
        z = lax.dot_general(q, kb, (((1,), (1,)), ((), ())), preferred_element_type=F32)
        sp = _softplus(z)
        if masked:
            kpos = ks + lax.broadcasted_iota(jnp.int32, (tq, tk), 1)
            qpos = q_pos0 + lax.broadcasted_iota(jnp.int32, (tq, tk), 0)
            before = kpos < qpos
            sp = jnp.where(before, sp, 0.0)
        mm = jnp.dot(sp.astype(BF16), u_ref[...], preferred_element_type=F32)
        car = car_scr[...]
        rest = mm[:, :tk] + jnp.concatenate([car] * (tk // LANES), axis=1)
        a = jnp.exp(z - rest)
        if masked:
            a = jnp.where(before, a, 0.0)
        acc_scr[...] += jnp.dot(a.astype(BF16), vb, preferred_element_type=F32)
        car_scr[...] = car + mm[:, tk:]

    def diag_body(t, c):
        block(n_all - 1 - t, True)
        return c

    def full_body(t, c):
        block(n_full - 1 - t, False)
        return c

    lax.fori_loop(0, n_all - n_full, diag_body, 0)
    lax.fori_loop(0, n_full, full_body, 0)
    o_ref[0] = acc_scr[...]


def _attention(q, k, v, q_start):
    g, t, d = q.shape
    tk_total = k.shape[1]
    tq, tk = min(ATTN_TQ, t), ATTN_TK
    jj = np.arange(tk)[:, None]
    ss = np.arange(tk + LANES)[None, :]
    u = jnp.asarray(((jj >= ss) | (ss >= tk)).astype(np.float32), BF16)
    kern = functools.partial(_attn_kernel, tq=tq, tk=tk, q_start=q_start)
    return pl.pallas_call(
        kern,
        grid=(g, t // tq),
        in_specs=[pl.BlockSpec((1, tq, d), lambda h, i: (h, i, 0)),
                  pl.BlockSpec((1, tk_total, d), lambda h, i: (h, 0, 0)),
                  pl.BlockSpec((1, tk_total, d), lambda h, i: (h, 0, 0)),
                  pl.BlockSpec(u.shape, lambda h, i: (0, 0))],
        out_specs=pl.BlockSpec((1, tq, d), lambda h, i: (h, i, 0)),
        out_shape=jax.ShapeDtypeStruct((g, t, d), F32),
        scratch_shapes=[pltpu.VMEM((tq, d), F32), pltpu.VMEM((tq, LANES), F32)],
        compiler_params=_params(("parallel", "arbitrary")),
        name="attn",
    )(q, k, v, u)


def _gdn_kernel(x_ref, z_ref, gt_ref, hist_ref, s0_ref, cw_ref, alog_ref, dtb_ref, ng_ref,
                o_ref, histo_ref, so_ref, xp_scr, s_scr):
    L = CHUNK
    c = pl.program_id(1)
    h0 = SUBLANES - (GDN_CONV - 1)

    @pl.when(c == 0)
    def _():
        xp_scr[h0:SUBLANES, :] = hist_ref[0]
        s_scr[...] = s0_ref[0]

    x = x_ref[0]
    xp_scr[SUBLANES:SUBLANES + L, :] = x
    cw = cw_ref[...]
    conv = x * cw[GDN_CONV - 1:GDN_CONV, :]
    for j in range(GDN_CONV - 1):
        conv = conv + xp_scr[h0 + j:h0 + j + L, :] * cw[j:j + 1, :]
    new_hist = xp_scr[L + h0:L + SUBLANES, :]
    xp_scr[h0:SUBLANES, :] = new_hist
    histo_ref[0] = new_hist
    act = conv * _sigmoid(conv)

    gates = gt_ref[0]
    beta_t = _sigmoid(gates)
    g_t = -jnp.exp(alog_ref[...]) * _softplus(gates + dtb_ref[...])
    row = lax.broadcasted_iota(jnp.int32, (L, L), 0)
    col = lax.broadcasted_iota(jnp.int32, (L, L), 1)
    tri = col <= row
    strict = col < row
    eye = (col == row).astype(F32)
    gc_t = _dot_exact_lhs(tri, g_t)
    egc_t = jnp.exp(gc_t)
    gc_tt = jnp.transpose(gc_t)
    z_all = z_ref[0]
    ng = ng_ref[...]

    outs = []
    for h in range(H_B):
        lo, hi = h * DH_B, (h + 1) * DH_B
        q = act[:, lo:hi]
        k = act[:, D_B + lo:D_B + hi]
        v = act[:, 2 * D_B + lo:2 * D_B + hi]
        q = q * lax.rsqrt(jnp.sum(q * q, axis=-1, keepdims=True) + EPS) * (DH_B ** -0.5)
        k = k * lax.rsqrt(jnp.sum(k * k, axis=-1, keepdims=True) + EPS)
        beta = beta_t[:, G_BETA + h:G_BETA + h + 1]
        gc = gc_t[:, G_A + h:G_A + h + 1]
        egc = egc_t[:, G_A + h:G_A + h + 1]
        gc_row = gc_tt[G_A + h:G_A + h + 1, :]
        gam = jnp.exp(jnp.where(tri, gc - gc_row, -jnp.inf))
        m = jnp.where(strict, beta * _dot_nt(k, k) * gam, 0.0)
        n = -m
        x_inv = eye + n
        p = n
        for _ in range(5):
            p = _dot3(p, p)
            x_inv = x_inv + _dot3(x_inv, p)
        u = _dot3(x_inv, beta * v)
        w = _dot3(x_inv, (beta * egc) * k)
        s = s_scr[h]
        v_new = u - _dot(w, s)
        qk = jnp.where(tri, _dot_nt(q, k) * gam, 0.0)
        o = _dot(q * egc, s) + _dot(qk, v_new)
        gl = gc[L - 1:L, :]
        kd = k * jnp.exp(gl - gc)
        s_scr[h] = s * jnp.exp(gl) + _dot(jnp.transpose(kd), v_new)
        o = _rms(o, ng) * (z_all[:, lo:hi] * _sigmoid(z_all[:, lo:hi]))
        outs.append(o)
    o_ref[0] = jnp.concatenate(outs, axis=-1)
    so_ref[0] = s_scr[...]


def _gdn(qkv, z, gates, hist, s0, conv_w, alog_row, dtb_row, ng_row):
    b, t, _ = qkv.shape
    nc = t // CHUNK
    blk = lambda bb, c: (bb, c, 0)
    per_b3 = lambda bb, c: (bb, 0, 0)
    per_b4 = lambda bb, c: (bb, 0, 0, 0)
    fixed = lambda bb, c: (0, 0)
    return pl.pallas_call(
        _gdn_kernel,
        grid=(b, nc),
        in_specs=[pl.BlockSpec((1, CHUNK, 3 * D_B), blk),
                  pl.BlockSpec((1, CHUNK, D_B), blk),
                  pl.BlockSpec((1, CHUNK, LANES), blk),
                  pl.BlockSpec((1, GDN_CONV - 1, 3 * D_B), per_b3),
                  pl.BlockSpec((1, H_B, DH_B, DH_B), per_b4),
                  pl.BlockSpec((GDN_CONV, 3 * D_B), fixed),
                  pl.BlockSpec((1, LANES), fixed),
                  pl.BlockSpec((1, LANES), fixed),
                  pl.BlockSpec((1, DH_B), fixed)],
        out_specs=[pl.BlockSpec((1, CHUNK, D_B), blk),
                   pl.BlockSpec((1, GDN_CONV - 1, 3 * D_B), per_b3),
                   pl.BlockSpec((1, H_B, DH_B, DH_B), per_b4)],
        out_shape=[jax.ShapeDtypeStruct((b, t, D_B), F32),
                   jax.ShapeDtypeStruct((b, GDN_CONV - 1, 3 * D_B), F32),
                   jax.ShapeDtypeStruct((b, H_B, DH_B, DH_B), F32)],
        scratch_shapes=[pltpu.VMEM((SUBLANES + CHUNK, 3 * D_B), F32),
                        pltpu.VMEM((H_B, DH_B, DH_B), F32)],
        compiler_params=_params(("parallel", "arbitrary")),
        name="gdn",
    )(qkv, z, gates, hist, s0, conv_w, alog_row, dtb_row, ng_row)


def _mlstm_kernel(x_ref, gt_ref, c0_ref, n0_ref, m0_ref, bi_ref, bf_ref, ng_ref,
                  o_ref, co_ref, no_ref, mo_ref, c_scr, n_scr, m_scr):
    L = CHUNK
    c = pl.program_id(1)

    @pl.when(c == 0)
    def _():
        c_scr[...] = c0_ref[0]
        n_scr[...] = n0_ref[0]
        m_scr[...] = m0_ref[0]

    gates = gt_ref[0]
    ig_t = gates + bi_ref[...]
    xf = gates + bf_ref[...]
    lf_t = jnp.minimum(xf, 0.0) - jnp.log(1.0 + jnp.exp(-jnp.abs(xf)))
    row = lax.broadcasted_iota(jnp.int32, (L, L), 0)
    col = lax.broadcasted_iota(jnp.int32, (L, L), 1)
    tri = col <= row
    bc_t = _dot_exact_lhs(tri, lf_t)
    bc_tt = jnp.transpose(bc_t)
    ig_tt = jnp.transpose(ig_t)
    lane = lax.broadcasted_iota(jnp.int32, (1, LANES), 1)
    m_all = m_scr[...]
    ng = ng_ref[...]

    for h in range(H_C):
        lo, hi = h * DH_C, (h + 1) * DH_C
        q = x_ref[0, :, lo:hi]
        k = x_ref[0, :, D_C + lo:D_C + hi] * (DH_C ** -0.5)
        v = x_ref[0, :, 2 * D_C + lo:2 * D_C + hi]
        og = x_ref[0, :, 3 * D_C + lo:3 * D_C + hi]
        bc = bc_t[:, G_F + h:G_F + h + 1]
        bc_row = bc_tt[G_F + h:G_F + h + 1, :]
        ig = ig_t[:, G_I + h:G_I + h + 1]
        ig_row = ig_tt[G_I + h:G_I + h + 1, :]
        m0 = m_all[:, h:h + 1]
        dmat = jnp.where(tri, bc - bc_row + ig_row, -jnp.inf)
        g = bc + m0
        m = jnp.maximum(g, jnp.max(dmat, axis=-1, keepdims=True))
        w = jnp.exp(dmat - m)
        inter = jnp.exp(g - m)
        qk = _dot_nt(q, k) * w
        cm = c_scr[h]
        nrow = n_scr[h:h + 1, :]
        num = inter * _dot(q, cm) + _dot(qk, v)
        den = inter * jnp.sum(q * nrow, axis=-1, keepdims=True) + jnp.sum(qk, axis=-1, keepdims=True)
        hh = num / jnp.maximum(jnp.abs(den), jnp.exp(-m))
        m_last = m[L - 1:L, :]
        decay = jnp.exp(g[L - 1:L, :] - m_last)
        wk = jnp.exp(bc[L - 1:L, :] - bc + ig - m_last)
        kd = k * wk
        c_scr[h] = decay * cm + _dot(jnp.transpose(kd), v)
        n_scr[h:h + 1, :] = decay * nrow + jnp.sum(kd, axis=0, keepdims=True)
        m_all = jnp.where(lane == h, m_last, m_all)
        o_ref[0, :, lo:hi] = _rms(hh, ng) * _sigmoid(og)
    m_scr[...] = m_all
    co_ref[0] = c_scr[...]
    no_ref[0] = n_scr[...]
    mo_ref[0] = m_all


def _mlstm(x, gates, c0, n0, m0, bi_row, bf_row, ng_row):
    b, t, _ = x.shape
    nc = t // CHUNK
    blk = lambda bb, c: (bb, c, 0)
    per_b3 = lambda bb, c: (bb, 0, 0)
    per_b4 = lambda bb, c: (bb, 0, 0, 0)
    fixed = lambda bb, c: (0, 0)
    return pl.pallas_call(
        _mlstm_kernel,
        grid=(b, nc),
        in_specs=[pl.BlockSpec((1, CHUNK, 4 * D_C), blk),
                  pl.BlockSpec((1, CHUNK, LANES), blk),
                  pl.BlockSpec((1, H_C, DH_C, DH_C), per_b4),
                  pl.BlockSpec((1, H_C, DH_C), per_b3),
                  pl.BlockSpec((1, 1, LANES), per_b3),
                  pl.BlockSpec((1, LANES), fixed),
                  pl.BlockSpec((1, LANES), fixed),
                  pl.BlockSpec((1, DH_C), fixed)],
        out_specs=[pl.BlockSpec((1, CHUNK, D_C), blk),
                   pl.BlockSpec((1, H_C, DH_C, DH_C), per_b4),
                   pl.BlockSpec((1, H_C, DH_C), per_b3),
                   pl.BlockSpec((1, 1, LANES), per_b3)],
        out_shape=[jax.ShapeDtypeStruct((b, t, D_C), F32),
                   jax.ShapeDtypeStruct((b, H_C, DH_C, DH_C), F32),
                   jax.ShapeDtypeStruct((b, H_C, DH_C), F32),
                   jax.ShapeDtypeStruct((b, 1, LANES), F32)],
        scratch_shapes=[pltpu.VMEM((H_C, DH_C, DH_C), F32),
                        pltpu.VMEM((H_C, DH_C), F32),
                        pltpu.VMEM((1, LANES), F32)],
        compiler_params=_params(("parallel", "arbitrary")),
        name="mlstm",
    )(x, gates, c0, n0, m0, bi_row, bf_row, ng_row)


def _outproj_kernel(x_ref, a_ref, b_ref, c_ref, w_ref, g_ref, o_ref):
    mix = (_dot(a_ref[...], w_ref[0:D_A, :])
           + _dot(b_ref[...], w_ref[D_A:D_A + D_B, :])
           + _dot(c_ref[...], w_ref[D_A + D_B:, :]))
    o_ref[...] = x_ref[...] + _rms(mix, g_ref[...])


def _outproj(x, oa, ob, oc, w_out, g):
    n = x.shape[0]
    r = min(ROW_TILE, n)
    row = lambda i: (i, 0)
    fixed = lambda i: (0, 0)
    return pl.pallas_call(
        _outproj_kernel,
        grid=(n // r,),
        in_specs=[pl.BlockSpec((r, D_MODEL), row),
                  pl.BlockSpec((r, D_A), row),
                  pl.BlockSpec((r, D_B), row),
                  pl.BlockSpec((r, D_C), row),
                  pl.BlockSpec(w_out.shape, fixed),
                  pl.BlockSpec((1, D_MODEL), fixed)],
        out_specs=pl.BlockSpec((r, D_MODEL), row),
        out_shape=jax.ShapeDtypeStruct((n, D_MODEL), F32),
        compiler_params=_params(("parallel",)),
        name="outproj",
    )(x, oa, ob, oc, w_out, g)


def _ffn_kernel(x_ref, hist_ref, g1_ref, wu_ref, cw_ref, wd_ref, g2_ref,
                o_ref, histo_ref, gp_scr, hist_scr, acc_scr, *, rows):
    t = pl.program_id(1)
    h0 = SUBLANES - (FFN_CONV - 1)

    @pl.when(t == 0)
    def _():
        hist_scr[h0:SUBLANES, :] = hist_ref[0]

    x = x_ref[0]
    h = _rms(x, g1_ref[...]).astype(BF16)
    acc_scr[...] = jnp.zeros_like(acc_scr)
    for ci in range(D_FF // FF_CHUNK):
        lo, hi = ci * FF_CHUNK, (ci + 1) * FF_CHUNK
        gate = jnp.dot(h, wu_ref[:, lo:hi], preferred_element_type=F32)
        up = jnp.dot(h, wu_ref[:, D_FF + lo:D_FF + hi], preferred_element_type=F32)
        gp_scr[h0:SUBLANES, :] = hist_scr[h0:SUBLANES, lo:hi]
        gp_scr[SUBLANES:SUBLANES + rows, :] = gate
        cw = cw_ref[:, lo:hi]
        conv = gate * cw[FFN_CONV - 1:FFN_CONV, :]
        for j in range(FFN_CONV - 1):
            conv = conv + gp_scr[h0 + j:h0 + j + rows, :] * cw[j:j + 1, :]
        hist_scr[h0:SUBLANES, lo:hi] = gp_scr[rows + h0:rows + SUBLANES, :]
        act = jax.nn.gelu(conv, approximate=True) * up
        acc_scr[...] += jnp.dot(act.astype(BF16), wd_ref[lo:hi, :], preferred_element_type=F32)
    histo_ref[0] = hist_scr[h0:SUBLANES, :]
    o_ref[0] = x + _rms(acc_scr[...], g2_ref[...])


def _ffn(x, hist, g1, w_up, conv_w, w_down, g2):
    b, t, _ = x.shape
    r = min(ROW_TILE, t)
    blk = lambda bb, i: (bb, i, 0)
    per_b = lambda bb, i: (bb, 0, 0)
    fixed = lambda bb, i: (0, 0)
    kern = functools.partial(_ffn_kernel, rows=r)
    return pl.pallas_call(
        kern,
        grid=(b, t // r),
        in_specs=[pl.BlockSpec((1, r, D_MODEL), blk),
                  pl.BlockSpec((1, FFN_CONV - 1, D_FF), per_b),
                  pl.BlockSpec((1, D_MODEL), fixed),
                  pl.BlockSpec(w_up.shape, fixed),
                  pl.BlockSpec((FFN_CONV, D_FF), fixed),
                  pl.BlockSpec(w_down.shape, fixed),
                  pl.BlockSpec((1, D_MODEL), fixed)],
        out_specs=[pl.BlockSpec((1, r, D_MODEL), blk),
                   pl.BlockSpec((1, FFN_CONV - 1, D_FF), per_b)],
        out_shape=[jax.ShapeDtypeStruct((b, t, D_MODEL), F32),
                   jax.ShapeDtypeStruct((b, FFN_CONV - 1, D_FF), F32)],
        scratch_shapes=[pltpu.VMEM((SUBLANES + r, FF_CHUNK), F32),
                        pltpu.VMEM((SUBLANES, D_FF), F32),
                        pltpu.VMEM((r, D_MODEL), F32)],
        compiler_params=_params(("parallel", "arbitrary")),
        name="ffn",
    )(x, hist, g1, w_up, conv_w, w_down, g2)


def _lane_row(vals, offset):
    return jnp.zeros((1, LANES), F32).at[0, offset:offset + vals.shape[0]].set(vals.astype(F32))


def _prep_weights(w_in, w_out, ffn_w_up, ffn_w_down):
    c_gate0 = 3 * D_A + 4 * D_B
    c_c0 = c_gate0 + 2 * H_B
    c_gate1 = c_c0 + 4 * D_C
    w_main = jnp.concatenate([w_in[:, :c_gate0], w_in[:, c_c0:c_gate1]], axis=1).astype(BF16)
    n_gate = 2 * H_B + 2 * H_C
    w_gate = jnp.concatenate([w_in[:, c_gate0:c_c0], w_in[:, c_gate1:],
                              jnp.zeros((D_MODEL, LANES - n_gate), w_in.dtype)], axis=1).astype(BF16)
    return w_main, w_gate, w_out.astype(BF16), ffn_w_up.astype(BF16), ffn_w_down.astype(BF16)


def _to_heads(x, b, t, h):
    return x.reshape(b, t, h, -1).transpose(0, 2, 1, 3)


def _layer(x, kv_k, kv_v, gdn_hist, gdn_s, m_c, m_n, m_m, ffn_hist, lw):
    (g_mix_pre, g_mix_post, g_ffn_pre, g_ffn_post, w_main, w_gate, gdn_conv_w, alog_row, dtb_row,
     gdn_ng, bi_row, bf_row, mlstm_ng, w_out, w_up, ffn_conv_w, w_down) = lw
    b, t, _ = x.shape
    n = b * t
    xf = x.reshape(n, D_MODEL)
    qkv_a, qkv_b, z_b, qkvo_c, gates = _inproj(xf, g_mix_pre, w_main, w_gate)

    k_new = _to_heads(qkv_a[:, D_A:2 * D_A], b, t, H_A)
    v_new = _to_heads(qkv_a[:, 2 * D_A:], b, t, H_A)
    q_h = _to_heads(qkv_a[:, :D_A], b, t, H_A).astype(BF16)
    past = 0 if kv_k is None else kv_k.shape[2]
    tq = min(ATTN_TQ, t)
    tk_total = -(-(past + t) // ATTN_TK) * ATTN_TK
    parts_k = [k_new.astype(BF16)] if kv_k is None else [kv_k.astype(BF16), k_new.astype(BF16)]
    parts_v = [v_new.astype(BF16)] if kv_v is None else [kv_v.astype(BF16), v_new.astype(BF16)]
    pad = tk_total - (past + t)
    if pad > 0:
        zpad = jnp.zeros((b, H_A, pad, DH_A), BF16)
        parts_k.append(zpad)
        parts_v.append(zpad)
    k_all = jnp.concatenate(parts_k, axis=2) if len(parts_k) > 1 else parts_k[0]
    v_all = jnp.concatenate(parts_v, axis=2) if len(parts_v) > 1 else parts_v[0]
    g = b * H_A
    o_a = _attention(q_h.reshape(g, t, DH_A), k_all.reshape(g, tk_total, DH_A),
                     v_all.reshape(g, tk_total, DH_A), past)
    o_a = o_a.reshape(b, H_A, t, DH_A).transpose(0, 2, 1, 3).reshape(n, D_A)

    o_b, gdn_hist_new, s_new = _gdn(qkv_b.reshape(b, t, 3 * D_B), z_b.reshape(b, t, D_B),
                                    gates.reshape(b, t, LANES), gdn_hist, gdn_s, gdn_conv_w,
                                    alog_row, dtb_row, gdn_ng)

    o_c, c_new, n_new, m_new = _mlstm(qkvo_c.reshape(b, t, 4 * D_C), gates.reshape(b, t, LANES),
                                      m_c, m_n, m_m, bi_row, bf_row, mlstm_ng)

    x1 = _outproj(xf, o_a, o_b.reshape(n, D_B), o_c.reshape(n, D_C), w_out, g_mix_post)
    x2, ffn_hist_new = _ffn(x1.reshape(b, t, D_MODEL), ffn_hist, g_ffn_pre, w_up, ffn_conv_w,
                            w_down, g_ffn_post)
    return (x2, k_new, v_new, gdn_hist_new, s_new, c_new, n_new, m_new[:, 0, :H_C], ffn_hist_new)


def kernel(x_prompt, x_sample, cache_sb_k, cache_sb_v, state_gdn_conv, state_gdn_s, state_mlstm_c, state_mlstm_n, state_mlstm_m, state_ffn_conv, g_mix_pre, g_mix_post, g_ffn_pre, g_ffn_post, w_in, gdn_conv_w, gdn_a_log, gdn_dt_bias, gdn_norm_g, mlstm_b_i, mlstm_b_f, mlstm_norm_g, w_out, ffn_w_up, ffn_conv_w, ffn_w_down):
    depth = w_in.shape[0]
    bp = x_prompt.shape[0]
    bs = x_sample.shape[0]
    zero_gdn_hist = jnp.zeros((bp, GDN_CONV - 1, 3 * D_B), F32)
    zero_s = jnp.zeros((bp, H_B, DH_B, DH_B), F32)
    zero_c = jnp.zeros((bp, H_C, DH_C, DH_C), F32)
    zero_n = jnp.zeros((bp, H_C, DH_C), F32)
    m_init = jnp.full((bp, 1, LANES), NEG, F32)
    zero_ffn_hist = jnp.zeros((bp, FFN_CONV - 1, D_FF), F32)

    xp, xs = x_prompt, x_sample
    new_p, new_s = [], []
    for l in range(depth):
        w_main, w_gate, w_o, w_u, w_d = _prep_weights(w_in[l], w_out[l], ffn_w_up[l], ffn_w_down[l])
        lw = (g_mix_pre[l][None], g_mix_post[l][None], g_ffn_pre[l][None], g_ffn_post[l][None],
              w_main, w_gate, gdn_conv_w[l], _lane_row(gdn_a_log[l], G_A),
              _lane_row(gdn_dt_bias[l], G_A), gdn_norm_g[l][None],
              _lane_row(mlstm_b_i[l], G_I), _lane_row(mlstm_b_f[l], G_F), mlstm_norm_g[l][None],
              w_o, w_u, ffn_conv_w[l], w_d)
        xp, *st_p = _layer(xp, None, None, zero_gdn_hist, zero_s, zero_c, zero_n, m_init,
                           zero_ffn_hist, lw)
        m_s = jnp.zeros((bs, 1, LANES), F32).at[:, 0, :H_C].set(state_mlstm_m[l])
        xs, *st_s = _layer(xs, cache_sb_k[l], cache_sb_v[l], state_gdn_conv[l], state_gdn_s[l],
                           state_mlstm_c[l], state_mlstm_n[l], m_s, state_ffn_conv[l], lw)
        new_p.append(st_p)
        new_s.append(st_s)
    outs_p = [jnp.stack(a) for a in zip(*new_p)]
    outs_s = [jnp.stack(a) for a in zip(*new_s)]
    return (xp, xs, *outs_p, *outs_s)
```

```python
import functools

import jax
import jax.numpy as jnp
import numpy as np
from jax import lax
from jax.experimental import pallas as pl
from jax.experimental.pallas import tpu as pltpu

F32 = jnp.float32
BF16 = jnp.bfloat16

D_MODEL = 1024
CHUNK = 64
H_A, DH_A = 4, 64
H_B, DH_B = 4, 64
H_C, DH_C = 4, 128
D_A, D_B, D_C = H_A * DH_A, H_B * DH_B, H_C * DH_C
GDN_CONV = 4
D_FF = 2816
FFN_CONV = 3
EPS = 1e-6
NEG = -1e30

LANES = 128
SUBLANES = 8
ROW_TILE = 512
ATTN_TQ = 1024
ATTN_TK = 256
FF_CHUNK = 256
VMEM_LIMIT = 56 * 1024 * 1024

G_BETA, G_A, G_I, G_F = 0, 4, 8, 12


def _params(sem):
    return pltpu.CompilerParams(dimension_semantics=sem, vmem_limit_bytes=VMEM_LIMIT)


def _dot(a, b):
    return jnp.dot(a.astype(BF16), b.astype(BF16), preferred_element_type=F32)


def _dot_nt(a, b):
    return lax.dot_general(a.astype(BF16), b.astype(BF16), (((1,), (1,)), ((), ())),
                           preferred_element_type=F32)


def _split(a):
    hi = a.astype(BF16)
    lo = (a - hi.astype(F32)).astype(BF16)
    return hi, lo


def _dot3(a, b):
    ah, al = _split(a)
    bh, bl = _split(b)
    return (jnp.dot(ah, bh, preferred_element_type=F32)
            + jnp.dot(ah, bl, preferred_element_type=F32)
            + jnp.dot(al, bh, preferred_element_type=F32))


def _dot_exact_lhs(a01, b):
    a = a01.astype(BF16)
    b0 = b.astype(BF16)
    r1 = b - b0.astype(F32)
    b1 = r1.astype(BF16)
    b2 = (r1 - b1.astype(F32)).astype(BF16)
    return (jnp.dot(a, b0, preferred_element_type=F32)
            + jnp.dot(a, b1, preferred_element_type=F32)
            + jnp.dot(a, b2, preferred_element_type=F32))


def _sigmoid(x):
    return 1.0 / (1.0 + jnp.exp(-x))


def _softplus(x):
    return jnp.maximum(x, 0.0) + jnp.log(1.0 + jnp.exp(-jnp.abs(x)))


def _rms(x, g):
    return x * lax.rsqrt(jnp.mean(x * x, axis=-1, keepdims=True) + EPS) * g


def _inproj_kernel(x_ref, g_ref, w_ref, wg_ref, a_ref, b_ref, z_ref, c_ref, gt_ref):
    h = _rms(x_ref[...], g_ref[...]).astype(BF16)
    a_ref[...] = jnp.dot(h, w_ref[:, 0:768], preferred_element_type=F32)
    b_ref[...] = jnp.dot(h, w_ref[:, 768:1536], preferred_element_type=F32)
    z_ref[...] = jnp.dot(h, w_ref[:, 1536:1792], preferred_element_type=F32)
    c_ref[...] = jnp.dot(h, w_ref[:, 1792:3840], preferred_element_type=F32)
    gt_ref[...] = jnp.dot(h, wg_ref[...], preferred_element_type=F32)


def _inproj(x, g, w_main, w_gate):
    n = x.shape[0]
    r = min(ROW_TILE, n)
    widths = (768, 768, 256, 2048, LANES)
    row = lambda i: (i, 0)
    fixed = lambda i: (0, 0)
    return pl.pallas_call(
        _inproj_kernel,
        grid=(n // r,),
        in_specs=[pl.BlockSpec((r, D_MODEL), row),
                  pl.BlockSpec((1, D_MODEL), fixed),
                  pl.BlockSpec(w_main.shape, fixed),
                  pl.BlockSpec(w_gate.shape, fixed)],
        out_specs=[pl.BlockSpec((r, w), row) for w in widths],
        out_shape=[jax.ShapeDtypeStruct((n, w), F32) for w in widths],
        compiler_params=_params(("parallel",)),
        name="inproj",
    )(x, g, w_main, w_gate)


def _attn_kernel(q_ref, kt_ref, v_ref, u_ref, o_ref, qs_scr, z_scr, sp_scr, a_scr, acc_scr, car_scr,
                 *, tq, tk, q_start):
    i = pl.program_id(1)
    q_pos0 = q_start + i * tq
    n_all = (q_pos0 + tq - 1 + tk - 1) // tk
    n_full = q_pos0 // tk
    n_blk = v_ref.shape[1] // tk
    qs_scr[...] = q_ref[0] * jnp.asarray(DH_A ** -0.5, BF16)
    acc_scr[...] = jnp.zeros_like(acc_scr)
    car_scr[...] = jnp.zeros_like(car_scr)
    z_scr[...] = jnp.full(z_scr.shape, NEG, F32)
    sp_scr[...] = jnp.zeros_like(sp_scr)
    a_scr[...] = jnp.zeros_like(a_scr)

    def stage_pv(j):
        ks = pl.multiple_of(jnp.clip(j, 0, n_blk - 1) * tk, tk)
        acc_scr[...] += jnp.dot(a_scr[...], v_ref[0, pl.ds(ks, tk), :], preferred_element_type=F32)

    def stage_exp():
        mm = jnp.dot(sp_scr[...], u_ref[...], preferred_element_type=F32)
        car = car_scr[...]
        rest = mm + jnp.concatenate([car] * (tk // LANES), axis=1)
        a_scr[...] = jnp.exp(z_scr[...] - rest).astype(BF16)
        car_scr[...] = car + jnp.broadcast_to(mm[:, 0:1], (tq, LANES))

    def stage_qk(j, masked):
        ks = pl.multiple_of(j * tk, tk)
        z = jnp.dot(qs_scr[...], kt_ref[0, :, pl.ds(ks, tk)], preferred_element_type=F32)
        if masked:
            kpos = ks + lax.broadcasted_iota(jnp.int32, (tq, tk), 1)
            qpos = q_pos0 + lax.broadcasted_iota(jnp.int32, (tq, tk), 0)
            z = jnp.where(kpos < qpos, z, NEG)
        z_scr[...] = z
        sp_scr[...] = _softplus(z).astype(BF16)

    def step(j, masked):
        stage_pv(j + 2)
        stage_exp()
        stage_qk(j, masked)

    def diag_body(t, c):
        step(n_all - 1 - t, True)
        return c

    def full_body(t, c):
        step(n_full - 1 - t, False)
        return c

    lax.fori_loop(0, n_all - n_full, diag_body, 0)
    lax.fori_loop(0, n_full, full_body, 0)
    stage_pv(1)
    stage_exp()
    stage_pv(0)
    o_ref[0] = acc_scr[...]


def _attention(q, kt, v, q_start):
    g, t, d = q.shape
    tk_total = v.shape[1]
    tq, tk = min(ATTN_TQ, t), ATTN_TK
    u = jnp.asarray((np.arange(tk)[:, None] >= np.arange(tk)[None, :]).astype(np.float32), BF16)
    kern = functools.partial(_attn_kernel, tq=tq, tk=tk, q_start=q_start)
    return pl.pallas_call(
        kern,
        grid=(g, t // tq),
        in_specs=[pl.BlockSpec((1, tq, d), lambda h, i: (h, i, 0)),
                  pl.BlockSpec((1, d, tk_total), lambda h, i: (h, 0, 0)),
                  pl.BlockSpec((1, tk_total, d), lambda h, i: (h, 0, 0)),
                  pl.BlockSpec(u.shape, lambda h, i: (0, 0))],
        out_specs=pl.BlockSpec((1, tq, d), lambda h, i: (h, i, 0)),
        out_shape=jax.ShapeDtypeStruct((g, t, d), F32),
        scratch_shapes=[pltpu.VMEM((tq, d), BF16), pltpu.VMEM((tq, tk), F32),
                        pltpu.VMEM((tq, tk), BF16), pltpu.VMEM((tq, tk), BF16),
                        pltpu.VMEM((tq, d), F32), pltpu.VMEM((tq, LANES), F32)],
        compiler_params=_params(("parallel", "arbitrary")),
        name="attn",
    )(q, kt, v, u)


def _gdn_kernel(x_ref, z_ref, gt_ref, hist_ref, s0_ref, cw_ref, alog_ref, dtb_ref, ng_ref,
                o_ref, histo_ref, so_ref, xp_scr, s_scr):
    L = CHUNK
    c = pl.program_id(1)
    h0 = SUBLANES - (GDN_CONV - 1)

    @pl.when(c == 0)
    def _():
        xp_scr[h0:SUBLANES, :] = hist_ref[0]
        s_scr[...] = s0_ref[0]

    x = x_ref[0]
    xp_scr[SUBLANES:SUBLANES + L, :] = x
    cw = cw_ref[...]
    conv = x * cw[GDN_CONV - 1:GDN_CONV, :]
    for j in range(GDN_CONV - 1):
        conv = conv + xp_scr[h0 + j:h0 + j + L, :] * cw[j:j + 1, :]
    new_hist = xp_scr[L + h0:L + SUBLANES, :]
    xp_scr[h0:SUBLANES, :] = new_hist
    histo_ref[0] = new_hist
    act = conv * _sigmoid(conv)

    gates = gt_ref[0]
    beta_t = _sigmoid(gates)
    g_t = -jnp.exp(alog_ref[...]) * _softplus(gates + dtb_ref[...])
    row = lax.broadcasted_iota(jnp.int32, (L, L), 0)
    col = lax.broadcasted_iota(jnp.int32, (L, L), 1)
    tri = col <= row
    strict = col < row
    eye = (col == row).astype(F32)
    gc_t = _dot_exact_lhs(tri, g_t)
    egc_t = jnp.exp(gc_t)
    gc_tt = jnp.transpose(gc_t)
    z_all = z_ref[0]
    ng = ng_ref[...]

    outs = []
    for h in range(H_B):
        lo, hi = h * DH_B, (h + 1) * DH_B
        q = act[:, lo:hi]
        k = act[:, D_B + lo:D_B + hi]
        v = act[:, 2 * D_B + lo:2 * D_B + hi]
        q = q * lax.rsqrt(jnp.sum(q * q, axis=-1, keepdims=True) + EPS) * (DH_B ** -0.5)
        k = k * lax.rsqrt(jnp.sum(k * k, axis=-1, keepdims=True) + EPS)
        beta = beta_t[:, G_BETA + h:G_BETA + h + 1]
        gc = gc_t[:, G_A + h:G_A + h + 1]
        egc = egc_t[:, G_A + h:G_A + h + 1]
        gc_row = gc_tt[G_A + h:G_A + h + 1, :]
        gam = jnp.exp(jnp.where(tri, gc - gc_row, -jnp.inf))
        m = jnp.where(strict, beta * _dot_nt(k, k) * gam, 0.0)
        n = -m
        x_inv = eye + n
        p = n
        for _ in range(5):
            p = _dot3(p, p)
            x_inv = x_inv + _dot3(x_inv, p)
        u = _dot3(x_inv, beta * v)
        w = _dot3(x_inv, (beta * egc) * k)
        s = s_scr[h]
        v_new = u - _dot(w, s)
        qk = jnp.where(tri, _dot_nt(q, k) * gam, 0.0)
        o = _dot(q * egc, s) + _dot(qk, v_new)
        gl = gc[L - 1:L, :]
        kd = k * jnp.exp(gl - gc)
        s_scr[h] = s * jnp.exp(gl) + _dot(jnp.transpose(kd), v_new)
        o = _rms(o, ng) * (z_all[:, lo:hi] * _sigmoid(z_all[:, lo:hi]))
        outs.append(o)
    o_ref[0] = jnp.concatenate(outs, axis=-1)
    so_ref[0] = s_scr[...]


def _gdn(qkv, z, gates, hist, s0, conv_w, alog_row, dtb_row, ng_row):
    b, t, _ = qkv.shape
    nc = t // CHUNK
    blk = lambda bb, c: (bb, c, 0)
    per_b3 = lambda bb, c: (bb, 0, 0)
    per_b4 = lambda bb, c: (bb, 0, 0, 0)
    fixed = lambda bb, c: (0, 0)
    return pl.pallas_call(
        _gdn_kernel,
        grid=(b, nc),
        in_specs=[pl.BlockSpec((1, CHUNK, 3 * D_B), blk),
                  pl.BlockSpec((1, CHUNK, D_B), blk),
                  pl.BlockSpec((1, CHUNK, LANES), blk),
                  pl.BlockSpec((1, GDN_CONV - 1, 3 * D_B), per_b3),
                  pl.BlockSpec((1, H_B, DH_B, DH_B), per_b4),
                  pl.BlockSpec((GDN_CONV, 3 * D_B), fixed),
                  pl.BlockSpec((1, LANES), fixed),
                  pl.BlockSpec((1, LANES), fixed),
                  pl.BlockSpec((1, DH_B), fixed)],
        out_specs=[pl.BlockSpec((1, CHUNK, D_B), blk),
                   pl.BlockSpec((1, GDN_CONV - 1, 3 * D_B), per_b3),
                   pl.BlockSpec((1, H_B, DH_B, DH_B), per_b4)],
        out_shape=[jax.ShapeDtypeStruct((b, t, D_B), F32),
                   jax.ShapeDtypeStruct((b, GDN_CONV - 1, 3 * D_B), F32),
                   jax.ShapeDtypeStruct((b, H_B, DH_B, DH_B), F32)],
        scratch_shapes=[pltpu.VMEM((SUBLANES + CHUNK, 3 * D_B), F32),
                        pltpu.VMEM((H_B, DH_B, DH_B), F32)],
        compiler_params=_params(("parallel", "arbitrary")),
        name="gdn",
    )(qkv, z, gates, hist, s0, conv_w, alog_row, dtb_row, ng_row)


def _mlstm_kernel(x_ref, gt_ref, c0_ref, n0_ref, m0_ref, bi_ref, bf_ref, ng_ref,
                  o_ref, co_ref, no_ref, mo_ref, c_scr, n_scr, m_scr):
    L = CHUNK
    c = pl.program_id(1)

    @pl.when(c == 0)
    def _():
        c_scr[...] = c0_ref[0]
        n_scr[...] = n0_ref[0]
        m_scr[...] = m0_ref[0]

    gates = gt_ref[0]
    ig_t = gates + bi_ref[...]
    xf = gates + bf_ref[...]
    lf_t = jnp.minimum(xf, 0.0) - jnp.log(1.0 + jnp.exp(-jnp.abs(xf)))
    row = lax.broadcasted_iota(jnp.int32, (L, L), 0)
    col = lax.broadcasted_iota(jnp.int32, (L, L), 1)
    tri = col <= row
    bc_t = _dot_exact_lhs(tri, lf_t)
    bc_tt = jnp.transpose(bc_t)
    ig_tt = jnp.transpose(ig_t)
    lane = lax.broadcasted_iota(jnp.int32, (1, LANES), 1)
    m_all = m_scr[...]
    ng = ng_ref[...]

    for h in range(H_C):
        lo, hi = h * DH_C, (h + 1) * DH_C
        q = x_ref[0, :, lo:hi]
        k = x_ref[0, :, D_C + lo:D_C + hi] * (DH_C ** -0.5)
        v = x_ref[0, :, 2 * D_C + lo:2 * D_C + hi]
        og = x_ref[0, :, 3 * D_C + lo:3 * D_C + hi]
        bc = bc_t[:, G_F + h:G_F + h + 1]
        bc_row = bc_tt[G_F + h:G_F + h + 1, :]
        ig = ig_t[:, G_I + h:G_I + h + 1]
        ig_row = ig_tt[G_I + h:G_I + h + 1, :]
        m0 = m_all[:, h:h + 1]
        dmat = jnp.where(tri, bc - bc_row + ig_row, -jnp.inf)
        g = bc + m0
        m = jnp.maximum(g, jnp.max(dmat, axis=-1, keepdims=True))
        w = jnp.exp(dmat - m)
        inter = jnp.exp(g - m)
        qk = _dot_nt(q, k) * w
        cm = c_scr[h]
        nrow = n_scr[h:h + 1, :]
        num = inter * _dot(q, cm) + _dot(qk, v)
        den = inter * jnp.sum(q * nrow, axis=-1, keepdims=True) + jnp.sum(qk, axis=-1, keepdims=True)
        hh = num / jnp.maximum(jnp.abs(den), jnp.exp(-m))
        m_last = m[L - 1:L, :]
        decay = jnp.exp(g[L - 1:L, :] - m_last)
        wk = jnp.exp(bc[L - 1:L, :] - bc + ig - m_last)
        kd = k * wk
        c_scr[h] = decay * cm + _dot(jnp.transpose(kd), v)
        n_scr[h:h + 1, :] = decay * nrow + jnp.sum(kd, axis=0, keepdims=True)
        m_all = jnp.where(lane == h, m_last, m_all)
        o_ref[0, :, lo:hi] = _rms(hh, ng) * _sigmoid(og)
    m_scr[...] = m_all
    co_ref[0] = c_scr[...]
    no_ref[0] = n_scr[...]
    mo_ref[0] = m_all


def _mlstm(x, gates, c0, n0, m0, bi_row, bf_row, ng_row):
    b, t, _ = x.shape
    nc = t // CHUNK
    blk = lambda bb, c: (bb, c, 0)
    per_b3 = lambda bb, c: (bb, 0, 0)
    per_b4 = lambda bb, c: (bb, 0, 0, 0)
    fixed = lambda bb, c: (0, 0)
    return pl.pallas_call(
        _mlstm_kernel,
        grid=(b, nc),
        in_specs=[pl.BlockSpec((1, CHUNK, 4 * D_C), blk),
                  pl.BlockSpec((1, CHUNK, LANES), blk),
                  pl.BlockSpec((1, H_C, DH_C, DH_C), per_b4),
                  pl.BlockSpec((1, H_C, DH_C), per_b3),
                  pl.BlockSpec((1, 1, LANES), per_b3),
                  pl.BlockSpec((1, LANES), fixed),
                  pl.BlockSpec((1, LANES), fixed),
                  pl.BlockSpec((1, DH_C), fixed)],
        out_specs=[pl.BlockSpec((1, CHUNK, D_C), blk),
                   pl.BlockSpec((1, H_C, DH_C, DH_C), per_b4),
                   pl.BlockSpec((1, H_C, DH_C), per_b3),
                   pl.BlockSpec((1, 1, LANES), per_b3)],
        out_shape=[jax.ShapeDtypeStruct((b, t, D_C), F32),
                   jax.ShapeDtypeStruct((b, H_C, DH_C, DH_C), F32),
                   jax.ShapeDtypeStruct((b, H_C, DH_C), F32),
                   jax.ShapeDtypeStruct((b, 1, LANES), F32)],
        scratch_shapes=[pltpu.VMEM((H_C, DH_C, DH_C), F32),
                        pltpu.VMEM((H_C, DH_C), F32),
                        pltpu.VMEM((1, LANES), F32)],
        compiler_params=_params(("parallel", "arbitrary")),
        name="mlstm",
    )(x, gates, c0, n0, m0, bi_row, bf_row, ng_row)


def _outproj_kernel(x_ref, a_ref, b_ref, c_ref, w_ref, g_ref, o_ref):
    mix = (_dot(a_ref[...], w_ref[0:D_A, :])
           + _dot(b_ref[...], w_ref[D_A:D_A + D_B, :])
           + _dot(c_ref[...], w_ref[D_A + D_B:, :]))
    o_ref[...] = x_ref[...] + _rms(mix, g_ref[...])


def _outproj(x, oa, ob, oc, w_out, g):
    n = x.shape[0]
    r = min(ROW_TILE, n)
    row = lambda i: (i, 0)
    fixed = lambda i: (0, 0)
    return pl.pallas_call(
        _outproj_kernel,
        grid=(n // r,),
        in_specs=[pl.BlockSpec((r, D_MODEL), row),
                  pl.BlockSpec((r, D_A), row),
                  pl.BlockSpec((r, D_B), row),
                  pl.BlockSpec((r, D_C), row),
                  pl.BlockSpec(w_out.shape, fixed),
                  pl.BlockSpec((1, D_MODEL), fixed)],
        out_specs=pl.BlockSpec((r, D_MODEL), row),
        out_shape=jax.ShapeDtypeStruct((n, D_MODEL), F32),
        compiler_params=_params(("parallel",)),
        name="outproj",
    )(x, oa, ob, oc, w_out, g)


def _ffn_kernel(x_ref, hist_ref, g1_ref, wu_ref, cw_ref, wd_ref, g2_ref,
                o_ref, histo_ref, gp_scr, hist_scr, acc_scr, *, rows):
    t = pl.program_id(1)
    h0 = SUBLANES - (FFN_CONV - 1)

    @pl.when(t == 0)
    def _():
        hist_scr[h0:SUBLANES, :] = hist_ref[0]

    x = x_ref[0]
    h = _rms(x, g1_ref[...]).astype(BF16)
    acc_scr[...] = jnp.zeros_like(acc_scr)
    for ci in range(D_FF // FF_CHUNK):
        lo, hi = ci * FF_CHUNK, (ci + 1) * FF_CHUNK
        gate = jnp.dot(h, wu_ref[:, lo:hi], preferred_element_type=F32)
        up = jnp.dot(h, wu_ref[:, D_FF + lo:D_FF + hi], preferred_element_type=F32)
        gp_scr[h0:SUBLANES, :] = hist_scr[h0:SUBLANES, lo:hi]
        gp_scr[SUBLANES:SUBLANES + rows, :] = gate
        cw = cw_ref[:, lo:hi]
        conv = gate * cw[FFN_CONV - 1:FFN_CONV, :]
        for j in range(FFN_CONV - 1):
            conv = conv + gp_scr[h0 + j:h0 + j + rows, :] * cw[j:j + 1, :]
        hist_scr[h0:SUBLANES, lo:hi] = gp_scr[rows + h0:rows + SUBLANES, :]
        act = jax.nn.gelu(conv, approximate=True) * up
        acc_scr[...] += jnp.dot(act.astype(BF16), wd_ref[lo:hi, :], preferred_element_type=F32)
    histo_ref[0] = hist_scr[h0:SUBLANES, :]
    o_ref[0] = x + _rms(acc_scr[...], g2_ref[...])


def _ffn(x, hist, g1, w_up, conv_w, w_down, g2):
    b, t, _ = x.shape
    r = min(ROW_TILE, t)
    blk = lambda bb, i: (bb, i, 0)
    per_b = lambda bb, i: (bb, 0, 0)
    fixed = lambda bb, i: (0, 0)
    kern = functools.partial(_ffn_kernel, rows=r)
    return pl.pallas_call(
        kern,
        grid=(b, t // r),
        in_specs=[pl.BlockSpec((1, r, D_MODEL), blk),
                  pl.BlockSpec((1, FFN_CONV - 1, D_FF), per_b),
                  pl.BlockSpec((1, D_MODEL), fixed),
                  pl.BlockSpec(w_up.shape, fixed),
                  pl.BlockSpec((FFN_CONV, D_FF), fixed),
                  pl.BlockSpec(w_down.shape, fixed),
                  pl.BlockSpec((1, D_MODEL), fixed)],
        out_specs=[pl.BlockSpec((1, r, D_MODEL), blk),
                   pl.BlockSpec((1, FFN_CONV - 1, D_FF), per_b)],
        out_shape=[jax.ShapeDtypeStruct((b, t, D_MODEL), F32),
                   jax.ShapeDtypeStruct((b, FFN_CONV - 1, D_FF), F32)],
        scratch_shapes=[pltpu.VMEM((SUBLANES + r, FF_CHUNK), F32),
                        pltpu.VMEM((SUBLANES, D_FF), F32),
                        pltpu.VMEM((r, D_MODEL), F32)],
        compiler_params=_params(("parallel", "arbitrary")),
        name="ffn",
    )(x, hist, g1, w_up, conv_w, w_down, g2)


def _lane_row(vals, offset):
    return jnp.zeros((1, LANES), F32).at[0, offset:offset + vals.shape[0]].set(vals.astype(F32))


def _prep_weights(w_in, w_out, ffn_w_up, ffn_w_down):
    c_gate0 = 3 * D_A + 4 * D_B
    c_c0 = c_gate0 + 2 * H_B
    c_gate1 = c_c0 + 4 * D_C
    w_main = jnp.concatenate([w_in[:, :c_gate0], w_in[:, c_c0:c_gate1]], axis=1).astype(BF16)
    n_gate = 2 * H_B + 2 * H_C
    w_gate = jnp.concatenate([w_in[:, c_gate0:c_c0], w_in[:, c_gate1:],
                              jnp.zeros((D_MODEL, LANES - n_gate), w_in.dtype)], axis=1).astype(BF16)
    return w_main, w_gate, w_out.astype(BF16), ffn_w_up.astype(BF16), ffn_w_down.astype(BF16)


def _to_heads(x, b, t, h):
    return x.reshape(b, t, h, -1).transpose(0, 2, 1, 3)


def _layer(x, kv_k, kv_v, gdn_hist, gdn_s, m_c, m_n, m_m, ffn_hist, lw):
    (g_mix_pre, g_mix_post, g_ffn_pre, g_ffn_post, w_main, w_gate, gdn_conv_w, alog_row, dtb_row,
     gdn_ng, bi_row, bf_row, mlstm_ng, w_out, w_up, ffn_conv_w, w_down) = lw
    b, t, _ = x.shape
    n = b * t
    xf = x.reshape(n, D_MODEL)
    qkv_a, qkv_b, z_b, qkvo_c, gates = _inproj(xf, g_mix_pre, w_main, w_gate)

    k_new = _to_heads(qkv_a[:, D_A:2 * D_A], b, t, H_A)
    v_new = _to_heads(qkv_a[:, 2 * D_A:], b, t, H_A)
    q_h = _to_heads(qkv_a[:, :D_A], b, t, H_A).astype(BF16)
    past = 0 if kv_k is None else kv_k.shape[2]
    tk_total = -(-(past + t) // ATTN_TK) * ATTN_TK
    parts_k = [k_new.astype(BF16)] if kv_k is None else [kv_k.astype(BF16), k_new.astype(BF16)]
    parts_v = [v_new.astype(BF16)] if kv_v is None else [kv_v.astype(BF16), v_new.astype(BF16)]
    pad = tk_total - (past + t)
    if pad > 0:
        zpad = jnp.zeros((b, H_A, pad, DH_A), BF16)
        parts_k.append(zpad)
        parts_v.append(zpad)
    k_all = jnp.concatenate(parts_k, axis=2) if len(parts_k) > 1 else parts_k[0]
    v_all = jnp.concatenate(parts_v, axis=2) if len(parts_v) > 1 else parts_v[0]
    g = b * H_A
    kt_all = k_all.transpose(0, 1, 3, 2)
    o_a = _attention(q_h.reshape(g, t, DH_A), kt_all.reshape(g, DH_A, tk_total),
                     v_all.reshape(g, tk_total, DH_A), past)
    o_a = o_a.reshape(b, H_A, t, DH_A).transpose(0, 2, 1, 3).reshape(n, D_A)

    o_b, gdn_hist_new, s_new = _gdn(qkv_b.reshape(b, t, 3 * D_B), z_b.reshape(b, t, D_B),
                                    gates.reshape(b, t, LANES), gdn_hist, gdn_s, gdn_conv_w,
                                    alog_row, dtb_row, gdn_ng)

    o_c, c_new, n_new, m_new = _mlstm(qkvo_c.reshape(b, t, 4 * D_C), gates.reshape(b, t, LANES),
                                      m_c, m_n, m_m, bi_row, bf_row, mlstm_ng)

    x1 = _outproj(xf, o_a, o_b.reshape(n, D_B), o_c.reshape(n, D_C), w_out, g_mix_post)
    x2, ffn_hist_new = _ffn(x1.reshape(b, t, D_MODEL), ffn_hist, g_ffn_pre, w_up, ffn_conv_w,
                            w_down, g_ffn_post)
    return (x2, k_new, v_new, gdn_hist_new, s_new, c_new, n_new, m_new[:, 0, :H_C], ffn_hist_new)


def kernel(x_prompt, x_sample, cache_sb_k, cache_sb_v, state_gdn_conv, state_gdn_s, state_mlstm_c, state_mlstm_n, state_mlstm_m, state_ffn_conv, g_mix_pre, g_mix_post, g_ffn_pre, g_ffn_post, w_in, gdn_conv_w, gdn_a_log, gdn_dt_bias, gdn_norm_g, mlstm_b_i, mlstm_b_f, mlstm_norm_g, w_out, ffn_w_up, ffn_conv_w, ffn_w_down):
    depth = w_in.shape[0]
    bp = x_prompt.shape[0]
    bs = x_sample.shape[0]
    zero_gdn_hist = jnp.zeros((bp, GDN_CONV - 1, 3 * D_B), F32)
    zero_s = jnp.zeros((bp, H_B, DH_B, DH_B), F32)
    zero_c = jnp.zeros((bp, H_C, DH_C, DH_C), F32)
    zero_n = jnp.zeros((bp, H_C, DH_C), F32)
    m_init = jnp.full((bp, 1, LANES), NEG, F32)
    zero_ffn_hist = jnp.zeros((bp, FFN_CONV - 1, D_FF), F32)

    xp, xs = x_prompt, x_sample
    new_p, new_s = [], []
    for l in range(depth):
        w_main, w_gate, w_o, w_u, w_d = _prep_weights(w_in[l], w_out[l], ffn_w_up[l], ffn_w_down[l])
        lw = (g_mix_pre[l][None], g_mix_post[l][None], g_ffn_pre[l][None], g_ffn_post[l][None],
              w_main, w_gate, gdn_conv_w[l], _lane_row(gdn_a_log[l], G_A),
              _lane_row(gdn_dt_bias[l], G_A), gdn_norm_g[l][None],
              _lane_row(mlstm_b_i[l], G_I), _lane_row(mlstm_b_f[l], G_F), mlstm_norm_g[l][None],
              w_o, w_u, ffn_conv_w[l], w_d)
        xp, *st_p = _layer(xp, None, None, zero_gdn_hist, zero_s, zero_c, zero_n, m_init,
                           zero_ffn_hist, lw)
        m_s = jnp.zeros((bs, 1, LANES), F32).at[:, 0, :H_C].set(state_mlstm_m[l])
        xs, *st_s = _layer(xs, cache_sb_k[l], cache_sb_v[l], state_gdn_conv[l], state_gdn_s[l],
                           state_mlstm_c[l], state_mlstm_n[l], m_s, state_ffn_conv[l], lw)
        new_p.append(st_p)
        new_s.append(st_s)
    outs_p = [jnp.stack(a) for a in zip(*new_p)]
    outs_s = [jnp.stack(a) for a in zip(*new_s)]
    return (xp, xs, *outs_p, *outs_s)
```

```python
import functools

import jax
import jax.numpy as jnp
import numpy as np
from jax import lax
from jax.experimental import pallas as pl
from jax.experimental.pallas import tpu as pltpu

F32 = jnp.float32
BF16 = jnp.bfloat16

D_MODEL = 1024
CHUNK = 64
H_A, DH_A = 4, 64
H_B, DH_B = 4, 64
H_C, DH_C = 4, 128
D_A, D_B, D_C = H_A * DH_A, H_B * DH_B, H_C * DH_C
GDN_CONV = 4
D_FF = 2816
FFN_CONV = 3
EPS = 1e-6
NEG = -1e30

LANES = 128
SUBLANES = 8
ROW_TILE = 512
ATTN_TQ = 1024
ATTN_TK = 256
FF_CHUNK = 256
GDN_CHUNKS = 4
MLSTM_CHUNKS = 4
VMEM_LIMIT = 56 * 1024 * 1024

G_BETA, G_A, G_I, G_F = 0, 4, 8, 12


def _params(sem):
    return pltpu.CompilerParams(dimension_semantics=sem, vmem_limit_bytes=VMEM_LIMIT)


def _dot(a, b):
    return jnp.dot(a.astype(BF16), b.astype(BF16), preferred_element_type=F32)


def _dot_nt(a, b):
    return lax.dot_general(a.astype(BF16), b.astype(BF16), (((1,), (1,)), ((), ())),
                           preferred_element_type=F32)


def _split(a):
    hi = a.astype(BF16)
    lo = (a - hi.astype(F32)).astype(BF16)
    return hi, lo


def _dot3(a, b):
    ah, al = _split(a)
    bh, bl = _split(b)
    return (jnp.dot(ah, bh, preferred_element_type=F32)
            + jnp.dot(ah, bl, preferred_element_type=F32)
            + jnp.dot(al, bh, preferred_element_type=F32))


def _dot_exact_lhs(a01, b):
    a = a01.astype(BF16)
    b0 = b.astype(BF16)
    r1 = b - b0.astype(F32)
    b1 = r1.astype(BF16)
    b2 = (r1 - b1.astype(F32)).astype(BF16)
    return (jnp.dot(a, b0, preferred_element_type=F32)
            + jnp.dot(a, b1, preferred_element_type=F32)
            + jnp.dot(a, b2, preferred_element_type=F32))


def _sigmoid(x):
    return 1.0 / (1.0 + jnp.exp(-x))


def _softplus(x):
    return jnp.maximum(x, 0.0) + jnp.log(1.0 + jnp.exp(-jnp.abs(x)))


def _rms(x, g):
    return x * lax.rsqrt(jnp.mean(x * x, axis=-1, keepdims=True) + EPS) * g


def _inproj_kernel(x_ref, g_ref, w_ref, wg_ref, a_ref, b_ref, z_ref, c_ref, gt_ref):
    h = _rms(x_ref[...], g_ref[...]).astype(BF16)
    a_ref[...] = jnp.dot(h, w_ref[:, 0:768], preferred_element_type=F32)
    b_ref[...] = jnp.dot(h, w_ref[:, 768:1536], preferred_element_type=F32)
    z_ref[...] = jnp.dot(h, w_ref[:, 1536:1792], preferred_element_type=F32)
    c_ref[...] = jnp.dot(h, w_ref[:, 1792:3840], preferred_element_type=F32)
    gt_ref[...] = jnp.dot(h, wg_ref[...], preferred_element_type=F32)


def _inproj(x, g, w_main, w_gate):
    n = x.shape[0]
    r = min(ROW_TILE, n)
    widths = (768, 768, 256, 2048, LANES)
    row = lambda i: (i, 0)
    fixed = lambda i: (0, 0)
    return pl.pallas_call(
        _inproj_kernel,
        grid=(n // r,),
        in_specs=[pl.BlockSpec((r, D_MODEL), row),
                  pl.BlockSpec((1, D_MODEL), fixed),
                  pl.BlockSpec(w_main.shape, fixed),
                  pl.BlockSpec(w_gate.shape, fixed)],
        out_specs=[pl.BlockSpec((r, w), row) for w in widths],
        out_shape=[jax.ShapeDtypeStruct((n, w), F32) for w in widths],
        compiler_params=_params(("parallel",)),
        name="inproj",
    )(x, g, w_main, w_gate)


def _attn_kernel(q_ref, kt_ref, v_ref, u_ref, o_ref, qs_scr, z_scr, sp_scr, a_scr, acc_scr, car_scr,
                 *, tq, tk, q_start):
    i = pl.program_id(1)
    q_pos0 = q_start + i * tq
    n_all = (q_pos0 + tq - 1 + tk - 1) // tk
    n_full = q_pos0 // tk
    n_blk = v_ref.shape[1] // tk
    qs_scr[...] = q_ref[0] * jnp.asarray(DH_A ** -0.5, BF16)
    acc_scr[...] = jnp.zeros_like(acc_scr)
    car_scr[...] = jnp.zeros_like(car_scr)
    z_scr[...] = jnp.full(z_scr.shape, NEG, BF16)
    sp_scr[...] = jnp.zeros_like(sp_scr)
    a_scr[...] = jnp.zeros_like(a_scr)

    def stage_qk(j, masked, slot):
        ks = pl.multiple_of(jnp.clip(j, 0, n_blk - 1) * tk, tk)
        z = jnp.dot(qs_scr[...], kt_ref[0, :, pl.ds(ks, tk)], preferred_element_type=F32)
        if masked:
            kpos = j * tk + lax.broadcasted_iota(jnp.int32, (tq, tk), 1)
            qpos = q_pos0 + lax.broadcasted_iota(jnp.int32, (tq, tk), 0)
            z = jnp.where(kpos < qpos, z, NEG)
        zb = z.astype(BF16)
        z_scr[slot] = zb
        sp_scr[slot] = _softplus(zb)

    def stage_exp(slot):
        mm = jnp.dot(sp_scr[slot], u_ref[...], preferred_element_type=F32)
        car = car_scr[...]
        rest = mm + jnp.concatenate([car] * (tk // LANES), axis=1)
        a_scr[slot] = jnp.exp(z_scr[slot] - rest.astype(BF16))
        car_scr[...] = car + jnp.broadcast_to(mm[:, 0:1], (tq, LANES))

    def stage_pv(j, slot):
        ks = pl.multiple_of(jnp.clip(j, 0, n_blk - 1) * tk, tk)
        acc_scr[...] += jnp.dot(a_scr[slot], v_ref[0, pl.ds(ks, tk), :], preferred_element_type=F32)

    def step(j, masked, slot):
        stage_qk(j, masked, slot)
        stage_exp(1 - slot)
        stage_pv(j + 2, slot)

    n_full2 = (n_full // 2) * 2
    n_diag2 = n_all - n_full2
    n_diag2 = n_diag2 + n_diag2 % 2

    def diag_body(t, c):
        j = n_full2 + n_diag2 - 1 - 2 * t
        step(j, True, 0)
        step(j - 1, True, 1)
        return c

    def full_body(t, c):
        j = n_full2 - 1 - 2 * t
        step(j, False, 0)
        step(j - 1, False, 1)
        return c

    lax.fori_loop(0, n_diag2 // 2, diag_body, 0)
    lax.fori_loop(0, n_full2 // 2, full_body, 0)
    stage_exp(1)
    stage_pv(1, 0)
    stage_pv(0, 1)
    o_ref[0] = acc_scr[...]


def _attention(q, kt, v, q_start):
    g, t, d = q.shape
    tk_total = v.shape[1]
    tq, tk = min(ATTN_TQ, t), ATTN_TK
    u = jnp.asarray((np.arange(tk)[:, None] >= np.arange(tk)[None, :]).astype(np.float32), BF16)
    kern = functools.partial(_attn_kernel, tq=tq, tk=tk, q_start=q_start)
    return pl.pallas_call(
        kern,
        grid=(g, t // tq),
        in_specs=[pl.BlockSpec((1, tq, d), lambda h, i: (h, i, 0)),
                  pl.BlockSpec((1, d, tk_total), lambda h, i: (h, 0, 0)),
                  pl.BlockSpec((1, tk_total, d), lambda h, i: (h, 0, 0)),
                  pl.BlockSpec(u.shape, lambda h, i: (0, 0))],
        out_specs=pl.BlockSpec((1, tq, d), lambda h, i: (h, i, 0)),
        out_shape=jax.ShapeDtypeStruct((g, t, d), F32),
        scratch_shapes=[pltpu.VMEM((tq, d), BF16), pltpu.VMEM((2, tq, tk), BF16),
                        pltpu.VMEM((2, tq, tk), BF16), pltpu.VMEM((2, tq, tk), BF16),
                        pltpu.VMEM((tq, d), F32), pltpu.VMEM((tq, LANES), F32)],
        compiler_params=_params(("parallel", "arbitrary")),
        name="attn",
    )(q, kt, v, u)


def _gdn_kernel(x_ref, z_ref, gt_ref, hist_ref, s0_ref, cw_ref, alog_ref, dtb_ref, ng_ref,
                o_ref, histo_ref, so_ref, xp_scr, s_scr, *, nc):
    L = CHUNK
    rows = nc * L
    step = pl.program_id(1)
    h0 = SUBLANES - (GDN_CONV - 1)

    @pl.when(step == 0)
    def _():
        xp_scr[h0:SUBLANES, :] = hist_ref[0]
        s_scr[...] = s0_ref[0]

    x = x_ref[0]
    xp_scr[SUBLANES:SUBLANES + rows, :] = x
    cw = cw_ref[...]
    conv = x * cw[GDN_CONV - 1:GDN_CONV, :]
    for j in range(GDN_CONV - 1):
        conv = conv + xp_scr[h0 + j:h0 + j + rows, :] * cw[j:j + 1, :]
    new_hist = xp_scr[rows + h0:rows + SUBLANES, :]
    xp_scr[h0:SUBLANES, :] = new_hist
    histo_ref[0] = new_hist
    act = conv * _sigmoid(conv)

    gates = gt_ref[0]
    beta_t = _sigmoid(gates)
    g_t = -jnp.exp(alog_ref[...]) * _softplus(gates + dtb_ref[...])
    rr = lax.broadcasted_iota(jnp.int32, (rows, rows), 0)
    cc = lax.broadcasted_iota(jnp.int32, (rows, rows), 1)
    same_chunk = (rr // L) == (cc // L)
    gc_t = _dot_exact_lhs((cc <= rr) & same_chunk, g_t)
    egc_t = jnp.exp(gc_t)
    gc_tt = jnp.transpose(gc_t)
    row = lax.broadcasted_iota(jnp.int32, (L, L), 0)
    col = lax.broadcasted_iota(jnp.int32, (L, L), 1)
    tri = col <= row
    strict = col < row
    eye = (col == row).astype(F32)
    z_all = z_ref[0]
    ng = ng_ref[...]

    inst = [(c, h) for c in range(nc) for h in range(H_B)]
    qs, ks, nn, bv, bk, gams, egls, kds = [], [], [], [], [], [], [], []
    for c, h in inst:
        r0, lo, hi = c * L, h * DH_B, (h + 1) * DH_B
        q = act[r0:r0 + L, lo:hi]
        k = act[r0:r0 + L, D_B + lo:D_B + hi]
        v = act[r0:r0 + L, 2 * D_B + lo:2 * D_B + hi]
        k = k * lax.rsqrt(jnp.sum(k * k, axis=-1, keepdims=True) + EPS)
        beta = beta_t[r0:r0 + L, G_BETA + h:G_BETA + h + 1]
        gc = gc_t[r0:r0 + L, G_A + h:G_A + h + 1]
        egc = egc_t[r0:r0 + L, G_A + h:G_A + h + 1]
        gc_row = gc_tt[G_A + h:G_A + h + 1, r0:r0 + L]
        gam = jnp.exp(jnp.where(tri, gc - gc_row, -jnp.inf))
        gl = gc[L - 1:L, :]
        q = q * lax.rsqrt(jnp.sum(q * q, axis=-1, keepdims=True) + EPS) * (DH_B ** -0.5)
        qs.append((q, q * egc))
        ks.append(k)
        gams.append(gam)
        nn.append(-jnp.where(strict, beta * _dot_nt(k, k) * gam, 0.0))
        bv.append(beta * v)
        bk.append((beta * egc) * k)
        egls.append(jnp.exp(gl))
        kds.append(jnp.transpose(k * jnp.exp(gl - gc)))
    x_inv = [eye + n for n in nn]
    pw = nn
    for _ in range(5):
        pw = [_dot3(p, p) for p in pw]
        x_inv = [xi + _dot3(xi, p) for xi, p in zip(x_inv, pw)]
    us = [_dot3(xi, b) for xi, b in zip(x_inv, bv)]
    ws = [_dot3(xi, b) for xi, b in zip(x_inv, bk)]
    qks = [jnp.where(tri, _dot_nt(q, k) * gam, 0.0) for (q, _), k, gam in zip(qs, ks, gams)]

    state = [s_scr[h] for h in range(H_B)]
    for c in range(nc):
        ids = [c * H_B + h for h in range(H_B)]
        v_new = [us[i] - _dot(ws[i], s) for i, s in zip(ids, state)]
        outs = [_dot(qs[i][1], s) + _dot(qks[i], vn) for i, s, vn in zip(ids, state, v_new)]
        state = [s * egls[i] + _dot(kds[i], vn) for i, s, vn in zip(ids, state, v_new)]
        zc = z_all[c * L:(c + 1) * L, :]
        gate = zc * _sigmoid(zc)
        o = jnp.concatenate([_rms(oh, ng) for oh in outs], axis=-1) * gate
        o_ref[0, c * L:(c + 1) * L, :] = o
    for h in range(H_B):
        s_scr[h] = state[h]
    so_ref[0] = s_scr[...]


def _gdn(qkv, z, gates, hist, s0, conv_w, alog_row, dtb_row, ng_row):
    b, t, _ = qkv.shape
    nc = min(GDN_CHUNKS, t // CHUNK)
    rows = nc * CHUNK
    blk = lambda bb, c: (bb, c, 0)
    per_b3 = lambda bb, c: (bb, 0, 0)
    per_b4 = lambda bb, c: (bb, 0, 0, 0)
    fixed = lambda bb, c: (0, 0)
    return pl.pallas_call(
        functools.partial(_gdn_kernel, nc=nc),
        grid=(b, t // rows),
        in_specs=[pl.BlockSpec((1, rows, 3 * D_B), blk),
                  pl.BlockSpec((1, rows, D_B), blk),
                  pl.BlockSpec((1, rows, LANES), blk),
                  pl.BlockSpec((1, GDN_CONV - 1, 3 * D_B), per_b3),
                  pl.BlockSpec((1, H_B, DH_B, DH_B), per_b4),
                  pl.BlockSpec((GDN_CONV, 3 * D_B), fixed),
                  pl.BlockSpec((1, LANES), fixed),
                  pl.BlockSpec((1, LANES), fixed),
                  pl.BlockSpec((1, DH_B), fixed)],
        out_specs=[pl.BlockSpec((1, rows, D_B), blk),
                   pl.BlockSpec((1, GDN_CONV - 1, 3 * D_B), per_b3),
                   pl.BlockSpec((1, H_B, DH_B, DH_B), per_b4)],
        out_shape=[jax.ShapeDtypeStruct((b, t, D_B), F32),
                   jax.ShapeDtypeStruct((b, GDN_CONV - 1, 3 * D_B), F32),
                   jax.ShapeDtypeStruct((b, H_B, DH_B, DH_B), F32)],
        scratch_shapes=[pltpu.VMEM((SUBLANES + rows, 3 * D_B), F32),
                        pltpu.VMEM((H_B, DH_B, DH_B), F32)],
        compiler_params=_params(("parallel", "arbitrary")),
        name="gdn",
    )(qkv, z, gates, hist, s0, conv_w, alog_row, dtb_row, ng_row)


def _mlstm_kernel(x_ref, gt_ref, c0_ref, n0_ref, m0_ref, bi_ref, bf_ref, ng_ref,
                  o_ref, co_ref, no_ref, mo_ref, c_scr, n_scr, m_scr, *, nc):
    L = CHUNK
    rows = nc * L
    step = pl.program_id(1)

    @pl.when(step == 0)
    def _():
        c_scr[...] = c0_ref[0]
        n_scr[...] = n0_ref[0]
        m_scr[...] = m0_ref[0]

    gates = gt_ref[0]
    ig_t = gates + bi_ref[...]
    xf = gates + bf_ref[...]
    lf_t = jnp.minimum(xf, 0.0) - jnp.log(1.0 + jnp.exp(-jnp.abs(xf)))
    rr = lax.broadcasted_iota(jnp.int32, (rows, rows), 0)
    cc = lax.broadcasted_iota(jnp.int32, (rows, rows), 1)
    same_chunk = (rr // L) == (cc // L)
    bc_t = _dot_exact_lhs((cc <= rr) & same_chunk, lf_t)
    bc_tt = jnp.transpose(bc_t)
    ig_tt = jnp.transpose(ig_t)
    row = lax.broadcasted_iota(jnp.int32, (L, L), 0)
    col = lax.broadcasted_iota(jnp.int32, (L, L), 1)
    tri = col <= row
    lane = lax.broadcasted_iota(jnp.int32, (1, LANES), 1)
    m_all = m_scr[...]
    ng = ng_ref[...]

    inst = [(c, h) for c in range(nc) for h in range(H_C)]

    def sl(c, h, part):
        return x_ref[0, c * L:(c + 1) * L, part * D_C + h * DH_C:part * D_C + (h + 1) * DH_C]

    bcs = [bc_t[c * L:(c + 1) * L, G_F + h:G_F + h + 1] for c, h in inst]
    igs = [ig_t[c * L:(c + 1) * L, G_I + h:G_I + h + 1] for c, h in inst]
    dmats = [jnp.where(tri, bc - bc_tt[G_F + h:G_F + h + 1, c * L:(c + 1) * L]
                       + ig_tt[G_I + h:G_I + h + 1, c * L:(c + 1) * L], -jnp.inf)
             for (c, h), bc in zip(inst, bcs)]
    dmaxs = [jnp.max(d, axis=-1, keepdims=True) for d in dmats]
    ks = [sl(c, h, 1) * (DH_C ** -0.5) for c, h in inst]
    qk_raw = [_dot_nt(sl(c, h, 0), k) for (c, h), k in zip(inst, ks)]

    gs, ms = [None] * len(inst), [None] * len(inst)
    for h in range(H_C):
        m0 = m_all[:, h:h + 1]
        for c in range(nc):
            i = c * H_C + h
            gs[i] = bcs[i] + m0
            ms[i] = jnp.maximum(gs[i], dmaxs[i])
            m0 = ms[i][L - 1:L, :]
        m_all = jnp.where(lane == h, m0, m_all)
    m_last = [m[L - 1:L, :] for m in ms]
    decays = [jnp.exp(g[L - 1:L, :] - ml) for g, ml in zip(gs, m_last)]
    kds = [k * jnp.exp(bc[L - 1:L, :] - bc + ig - ml) for k, bc, ig, ml in zip(ks, bcs, igs, m_last)]
    kvs = [_dot(jnp.transpose(kd), sl(c, h, 2)) for (c, h), kd in zip(inst, kds)]
    ksums = [jnp.sum(kd, axis=0, keepdims=True) for kd in kds]

    cms, nrows = [None] * len(inst), [None] * len(inst)
    for h in range(H_C):
        cm, nrow = c_scr[h], n_scr[h:h + 1, :]
        for c in range(nc):
            i = c * H_C + h
            cms[i], nrows[i] = cm, nrow
            cm = decays[i] * cm + kvs[i]
            nrow = decays[i] * nrow + ksums[i]
        c_scr[h] = cm
        n_scr[h:h + 1, :] = nrow

    inters = [jnp.exp(g - m) for g, m in zip(gs, ms)]
    qks = [r * jnp.exp(d - m) for r, d, m in zip(qk_raw, dmats, ms)]
    nums = [it * _dot(sl(c, h, 0), cm) + _dot(qk, sl(c, h, 2))
            for (c, h), it, cm, qk in zip(inst, inters, cms, qks)]
    qns = [jnp.sum(sl(c, h, 0) * nrow, axis=-1, keepdims=True) for (c, h), nrow in zip(inst, nrows)]
    qksums = [jnp.sum(qk, axis=-1, keepdims=True) for qk in qks]
    dens = [it * qn + s for it, qn, s in zip(inters, qns, qksums)]
    hhs = [num / jnp.maximum(jnp.abs(den), jnp.exp(-m)) for num, den, m in zip(nums, dens, ms)]
    mss = [jnp.mean(hh * hh, axis=-1, keepdims=True) for hh in hhs]
    for (c, h), hh, msq in zip(inst, hhs, mss):
        o_ref[0, c * L:(c + 1) * L, h * DH_C:(h + 1) * DH_C] = (
            hh * lax.rsqrt(msq + EPS) * ng * _sigmoid(sl(c, h, 3)))
    m_scr[...] = m_all
    co_ref[0] = c_scr[...]
    no_ref[0] = n_scr[...]
    mo_ref[0] = m_all


def _mlstm(x, gates, c0, n0, m0, bi_row, bf_row, ng_row):
    b, t, _ = x.shape
    nc = min(MLSTM_CHUNKS, t // CHUNK)
    rows = nc * CHUNK
    blk = lambda bb, c: (bb, c, 0)
    per_b3 = lambda bb, c: (bb, 0, 0)
    per_b4 = lambda bb, c: (bb, 0, 0, 0)
    fixed = lambda bb, c: (0, 0)
    return pl.pallas_call(
        functools.partial(_mlstm_kernel, nc=nc),
        grid=(b, t // rows),
        in_specs=[pl.BlockSpec((1, rows, 4 * D_C), blk),
                  pl.BlockSpec((1, rows, LANES), blk),
                  pl.BlockSpec((1, H_C, DH_C, DH_C), per_b4),
                  pl.BlockSpec((1, H_C, DH_C), per_b3),
                  pl.BlockSpec((1, 1, LANES), per_b3),
                  pl.BlockSpec((1, LANES), fixed),
                  pl.BlockSpec((1, LANES), fixed),
                  pl.BlockSpec((1, DH_C), fixed)],
        out_specs=[pl.BlockSpec((1, rows, D_C), blk),
                   pl.BlockSpec((1, H_C, DH_C, DH_C), per_b4),
                   pl.BlockSpec((1, H_C, DH_C), per_b3),
                   pl.BlockSpec((1, 1, LANES), per_b3)],
        out_shape=[jax.ShapeDtypeStruct((b, t, D_C), F32),
                   jax.ShapeDtypeStruct((b, H_C, DH_C, DH_C), F32),
                   jax.ShapeDtypeStruct((b, H_C, DH_C), F32),
                   jax.ShapeDtypeStruct((b, 1, LANES), F32)],
        scratch_shapes=[pltpu.VMEM((H_C, DH_C, DH_C), F32),
                        pltpu.VMEM((H_C, DH_C), F32),
                        pltpu.VMEM((1, LANES), F32)],
        compiler_params=_params(("parallel", "arbitrary")),
        name="mlstm",
    )(x, gates, c0, n0, m0, bi_row, bf_row, ng_row)


def _outproj_kernel(x_ref, a_ref, b_ref, c_ref, w_ref, g_ref, o_ref):
    mix = (_dot(a_ref[...], w_ref[0:D_A, :])
           + _dot(b_ref[...], w_ref[D_A:D_A + D_B, :])
           + _dot(c_ref[...], w_ref[D_A + D_B:, :]))
    o_ref[...] = x_ref[...] + _rms(mix, g_ref[...])


def _outproj(x, oa, ob, oc, w_out, g):
    n = x.shape[0]
    r = min(ROW_TILE, n)
    row = lambda i: (i, 0)
    fixed = lambda i: (0, 0)
    return pl.pallas_call(
        _outproj_kernel,
        grid=(n // r,),
        in_specs=[pl.BlockSpec((r, D_MODEL), row),
                  pl.BlockSpec((r, D_A), row),
                  pl.BlockSpec((r, D_B), row),
                  pl.BlockSpec((r, D_C), row),
                  pl.BlockSpec(w_out.shape, fixed),
                  pl.BlockSpec((1, D_MODEL), fixed)],
        out_specs=pl.BlockSpec((r, D_MODEL), row),
        out_shape=jax.ShapeDtypeStruct((n, D_MODEL), F32),
        compiler_params=_params(("parallel",)),
        name="outproj",
    )(x, oa, ob, oc, w_out, g)


def _ffn_kernel(x_ref, hist_ref, g1_ref, wu_ref, cw_ref, wd_ref, g2_ref,
                o_ref, histo_ref, gp_scr, hist_scr, acc_scr, *, rows):
    t = pl.program_id(1)
    h0 = SUBLANES - (FFN_CONV - 1)

    @pl.when(t == 0)
    def _():
        hist_scr[h0:SUBLANES, :] = hist_ref[0]

    x = x_ref[0]
    h = _rms(x, g1_ref[...]).astype(BF16)
    acc_scr[...] = jnp.zeros_like(acc_scr)
    for ci in range(D_FF // FF_CHUNK):
        lo, hi = ci * FF_CHUNK, (ci + 1) * FF_CHUNK
        gate = jnp.dot(h, wu_ref[:, lo:hi], preferred_element_type=F32)
        up = jnp.dot(h, wu_ref[:, D_FF + lo:D_FF + hi], preferred_element_type=F32)
        gp_scr[h0:SUBLANES, :] = hist_scr[h0:SUBLANES, lo:hi]
        gp_scr[SUBLANES:SUBLANES + rows, :] = gate
        cw = cw_ref[:, lo:hi]
        conv = gate * cw[FFN_CONV - 1:FFN_CONV, :]
        for j in range(FFN_CONV - 1):
            conv = conv + gp_scr[h0 + j:h0 + j + rows, :] * cw[j:j + 1, :]
        hist_scr[h0:SUBLANES, lo:hi] = gp_scr[rows + h0:rows + SUBLANES, :]
        act = jax.nn.gelu(conv, approximate=True) * up
        acc_scr[...] += jnp.dot(act.astype(BF16), wd_ref[lo:hi, :], preferred_element_type=F32)
    histo_ref[0] = hist_scr[h0:SUBLANES, :]
    o_ref[0] = x + _rms(acc_scr[...], g2_ref[...])


def _ffn(x, hist, g1, w_up, conv_w, w_down, g2):
    b, t, _ = x.shape
    r = min(ROW_TILE, t)
    blk = lambda bb, i: (bb, i, 0)
    per_b = lambda bb, i: (bb, 0, 0)
    fixed = lambda bb, i: (0, 0)
    kern = functools.partial(_ffn_kernel, rows=r)
    return pl.pallas_call(
        kern,
        grid=(b, t // r),
        in_specs=[pl.BlockSpec((1, r, D_MODEL), blk),
                  pl.BlockSpec((1, FFN_CONV - 1, D_FF), per_b),
                  pl.BlockSpec((1, D_MODEL), fixed),
                  pl.BlockSpec(w_up.shape, fixed),
                  pl.BlockSpec((FFN_CONV, D_FF), fixed),
                  pl.BlockSpec(w_down.shape, fixed),
                  pl.BlockSpec((1, D_MODEL), fixed)],
        out_specs=[pl.BlockSpec((1, r, D_MODEL), blk),
                   pl.BlockSpec((1, FFN_CONV - 1, D_FF), per_b)],
        out_shape=[jax.ShapeDtypeStruct((b, t, D_MODEL), F32),
                   jax.ShapeDtypeStruct((b, FFN_CONV - 1, D_FF), F32)],
        scratch_shapes=[pltpu.VMEM((SUBLANES + r, FF_CHUNK), F32),
                        pltpu.VMEM((SUBLANES, D_FF), F32),
                        pltpu.VMEM((r, D_MODEL), F32)],
        compiler_params=_params(("parallel", "arbitrary")),
        name="ffn",
    )(x, hist, g1, w_up, conv_w, w_down, g2)


def _lane_row(vals, offset):
    return jnp.zeros((1, LANES), F32).at[0, offset:offset + vals.shape[0]].set(vals.astype(F32))


def _prep_weights(w_in, w_out, ffn_w_up, ffn_w_down):
    c_gate0 = 3 * D_A + 4 * D_B
    c_c0 = c_gate0 + 2 * H_B
    c_gate1 = c_c0 + 4 * D_C
    w_main = jnp.concatenate([w_in[:, :c_gate0], w_in[:, c_c0:c_gate1]], axis=1).astype(BF16)
    n_gate = 2 * H_B + 2 * H_C
    w_gate = jnp.concatenate([w_in[:, c_gate0:c_c0], w_in[:, c_gate1:],
                              jnp.zeros((D_MODEL, LANES - n_gate), w_in.dtype)], axis=1).astype(BF16)
    return w_main, w_gate, w_out.astype(BF16), ffn_w_up.astype(BF16), ffn_w_down.astype(BF16)


def _to_heads(x, b, t, h):
    return x.reshape(b, t, h, -1).transpose(0, 2, 1, 3)


def _layer(x, kv_k, kv_v, gdn_hist, gdn_s, m_c, m_n, m_m, ffn_hist, lw):
    (g_mix_pre, g_mix_post, g_ffn_pre, g_ffn_post, w_main, w_gate, gdn_conv_w, alog_row, dtb_row,
     gdn_ng, bi_row, bf_row, mlstm_ng, w_out, w_up, ffn_conv_w, w_down) = lw
    b, t, _ = x.shape
    n = b * t
    xf = x.reshape(n, D_MODEL)
    qkv_a, qkv_b, z_b, qkvo_c, gates = _inproj(xf, g_mix_pre, w_main, w_gate)

    k_new = _to_heads(qkv_a[:, D_A:2 * D_A], b, t, H_A)
    v_new = _to_heads(qkv_a[:, 2 * D_A:], b, t, H_A)
    q_h = _to_heads(qkv_a[:, :D_A], b, t, H_A).astype(BF16)
    past = 0 if kv_k is None else kv_k.shape[2]
    tk_total = -(-(past + t) // ATTN_TK) * ATTN_TK
    parts_k = [k_new.astype(BF16)] if kv_k is None else [kv_k.astype(BF16), k_new.astype(BF16)]
    parts_v = [v_new.astype(BF16)] if kv_v is None else [kv_v.astype(BF16), v_new.astype(BF16)]
    pad = tk_total - (past + t)
    if pad > 0:
        zpad = jnp.zeros((b, H_A, pad, DH_A), BF16)
        parts_k.append(zpad)
        parts_v.append(zpad)
    k_all = jnp.concatenate(parts_k, axis=2) if len(parts_k) > 1 else parts_k[0]
    v_all = jnp.concatenate(parts_v, axis=2) if len(parts_v) > 1 else parts_v[0]
    g = b * H_A
    kt_all = k_all.transpose(0, 1, 3, 2)
    o_a = _attention(q_h.reshape(g, t, DH_A), kt_all.reshape(g, DH_A, tk_total),
                     v_all.reshape(g, tk_total, DH_A), past)
    o_a = o_a.reshape(b, H_A, t, DH_A).transpose(0, 2, 1, 3).reshape(n, D_A)

    o_b, gdn_hist_new, s_new = _gdn(qkv_b.reshape(b, t, 3 * D_B), z_b.reshape(b, t, D_B),
                                    gates.reshape(b, t, LANES), gdn_hist, gdn_s, gdn_conv_w,
                                    alog_row, dtb_row, gdn_ng)

    o_c, c_new, n_new, m_new = _mlstm(qkvo_c.reshape(b, t, 4 * D_C), gates.reshape(b, t, LANES),
                                      m_c, m_n, m_m, bi_row, bf_row, mlstm_ng)

    x1 = _outproj(xf, o_a, o_b.reshape(n, D_B), o_c.reshape(n, D_C), w_out, g_mix_post)
    x2, ffn_hist_new = _ffn(x1.reshape(b, t, D_MODEL), ffn_hist, g_ffn_pre, w_up, ffn_conv_w,
                            w_down, g_ffn_post)
    return (x2, k_new, v_new, gdn_hist_new, s_new, c_new, n_new, m_new[:, 0, :H_C], ffn_hist_new)


def kernel(x_prompt, x_sample, cache_sb_k, cache_sb_v, state_gdn_conv, state_gdn_s, state_mlstm_c, state_mlstm_n, state_mlstm_m, state_ffn_conv, g_mix_pre, g_mix_post, g_ffn_pre, g_ffn_post, w_in, gdn_conv_w, gdn_a_log, gdn_dt_bias, gdn_norm_g, mlstm_b_i, mlstm_b_f, mlstm_norm_g, w_out, ffn_w_up, ffn_conv_w, ffn_w_down):
    depth = w_in.shape[0]
    bp = x_prompt.shape[0]
    bs = x_sample.shape[0]
    zero_gdn_hist = jnp.zeros((bp, GDN_CONV - 1, 3 * D_B), F32)
    zero_s = jnp.zeros((bp, H_B, DH_B, DH_B), F32)
    zero_c = jnp.zeros((bp, H_C, DH_C, DH_C), F32)
    zero_n = jnp.zeros((bp, H_C, DH_C), F32)
    m_init = jnp.full((bp, 1, LANES), NEG, F32)
    zero_ffn_hist = jnp.zeros((bp, FFN_CONV - 1, D_FF), F32)

    xp, xs = x_prompt, x_sample
    new_p, new_s = [], []
    for l in range(depth):
        w_main, w_gate, w_o, w_u, w_d = _prep_weights(w_in[l], w_out[l], ffn_w_up[l], ffn_w_down[l])
        lw = (g_mix_pre[l][None], g_mix_post[l][None], g_ffn_pre[l][None], g_ffn_post[l][None],
              w_main, w_gate, gdn_conv_w[l], _lane_row(gdn_a_log[l], G_A),
              _lane_row(gdn_dt_bias[l], G_A), gdn_norm_g[l][None],
              _lane_row(mlstm_b_i[l], G_I), _lane_row(mlstm_b_f[l], G_F), mlstm_norm_g[l][None],
              w_o, w_u, ffn_conv_w[l], w_d)
        xp, *st_p = _layer(xp, None, None, zero_gdn_hist, zero_s, zero_c, zero_n, m_init,
                           zero_ffn_hist, lw)
        m_s = jnp.zeros((bs, 1, LANES), F32).at[:, 0, :H_C].set(state_mlstm_m[l])
        xs, *st_s = _layer(xs, cache_sb_k[l], cache_sb_v[l], state_gdn_conv[l], state_gdn_s[l],
                           state_mlstm_c[l], state_mlstm_n[l], m_s, state_ffn_conv[l], lw)
        new_p.append(st_p)
        new_s.append(st_s)
    outs_p = [jnp.stack(a) for a in zip(*new_p)]
    outs_s = [jnp.stack(a) for a in zip(*new_s)]
    return (xp, xs, *outs_p, *outs_s)
```

```python
import functools

import jax
import jax.numpy as jnp
import numpy as np
from jax import lax
from jax.experimental import pallas as pl
from jax.experimental.pallas import tpu as pltpu

F32 = jnp.float32
BF16 = jnp.bfloat16

D_MODEL = 1024
CHUNK = 64
H_A, DH_A = 4, 64
H_B, DH_B = 4, 64
H_C, DH_C = 4, 128
D_A, D_B, D_C = H_A * DH_A, H_B * DH_B, H_C * DH_C
GDN_CONV = 4
D_FF = 2816
FFN_CONV = 3
EPS = 1e-6
NEG = -1e30

LANES = 128
SUBLANES = 8
ROW_TILE = 512
ATTN_TQ = 1024
ATTN_TK = 256
FF_CHUNK = 256
GDN_CHUNKS = 4
MLSTM_CHUNKS = 4
VMEM_LIMIT = 56 * 1024 * 1024

G_BETA, G_A, G_I, G_F = 0, 4, 8, 12


def _params(sem):
    return pltpu.CompilerParams(dimension_semantics=sem, vmem_limit_bytes=VMEM_LIMIT)


def _dot(a, b):
    return jnp.dot(a.astype(BF16), b.astype(BF16), preferred_element_type=F32)


def _dot_nt(a, b):
    return lax.dot_general(a.astype(BF16), b.astype(BF16), (((1,), (1,)), ((), ())),
                           preferred_element_type=F32)


def _split(a):
    hi = a.astype(BF16)
    lo = (a - hi.astype(F32)).astype(BF16)
    return hi, lo


def _dot3(a, b):
    ah, al = _split(a)
    bh, bl = _split(b)
    return (jnp.dot(ah, bh, preferred_element_type=F32)
            + jnp.dot(ah, bl, preferred_element_type=F32)
            + jnp.dot(al, bh, preferred_element_type=F32))


def _dot_exact_lhs(a01, b):
    a = a01.astype(BF16)
    b0 = b.astype(BF16)
    r1 = b - b0.astype(F32)
    b1 = r1.astype(BF16)
    b2 = (r1 - b1.astype(F32)).astype(BF16)
    return (jnp.dot(a, b0, preferred_element_type=F32)
            + jnp.dot(a, b1, preferred_element_type=F32)
            + jnp.dot(a, b2, preferred_element_type=F32))


def _sigmoid(x):
    return 1.0 / (1.0 + jnp.exp(-x))


def _softplus(x):
    return jnp.maximum(x, 0.0) + jnp.log(1.0 + jnp.exp(-jnp.abs(x)))


def _rms(x, g):
    return x * lax.rsqrt(jnp.mean(x * x, axis=-1, keepdims=True) + EPS) * g


def _inproj_kernel(x_ref, g_ref, w_ref, wg_ref, a_ref, b_ref, z_ref, c_ref, gt_ref):
    h = _rms(x_ref[...], g_ref[...]).astype(BF16)
    a_ref[...] = jnp.dot(h, w_ref[:, 0:768], preferred_element_type=F32)
    b_ref[...] = jnp.dot(h, w_ref[:, 768:1536], preferred_element_type=F32)
    z_ref[...] = jnp.dot(h, w_ref[:, 1536:1792], preferred_element_type=F32)
    c_ref[...] = jnp.dot(h, w_ref[:, 1792:3840], preferred_element_type=F32)
    gt_ref[...] = jnp.dot(h, wg_ref[...], preferred_element_type=F32)


def _inproj(x, g, w_main, w_gate):
    n = x.shape[0]
    r = min(ROW_TILE, n)
    widths = (768, 768, 256, 2048, LANES)
    row = lambda i: (i, 0)
    fixed = lambda i: (0, 0)
    return pl.pallas_call(
        _inproj_kernel,
        grid=(n // r,),
        in_specs=[pl.BlockSpec((r, D_MODEL), row),
                  pl.BlockSpec((1, D_MODEL), fixed),
                  pl.BlockSpec(w_main.shape, fixed),
                  pl.BlockSpec(w_gate.shape, fixed)],
        out_specs=[pl.BlockSpec((r, w), row) for w in widths],
        out_shape=[jax.ShapeDtypeStruct((n, w), F32) for w in widths],
        compiler_params=_params(("parallel",)),
        name="inproj",
    )(x, g, w_main, w_gate)


def _attn_kernel(q_ref, kt_ref, v_ref, u_ref, o_ref, qs_scr, z_scr, sp_scr, a_scr, acc_scr, car_scr,
                 *, tq, tk, q_start):
    i = pl.program_id(1)
    q_pos0 = q_start + i * tq
    n_all = (q_pos0 + tq - 1 + tk - 1) // tk
    n_full = q_pos0 // tk
    n_blk = v_ref.shape[1] // tk
    qs_scr[...] = q_ref[0] * jnp.asarray(DH_A ** -0.5, BF16)
    acc_scr[...] = jnp.zeros_like(acc_scr)
    car_scr[...] = jnp.zeros_like(car_scr)
    z_scr[1] = jnp.full((tq, tk), NEG, BF16)
    sp_scr[1] = jnp.zeros((tq, tk), BF16)
    a_scr[0] = jnp.zeros((tq, tk), BF16)

    def stage_qk(j, masked, slot):
        ks = pl.multiple_of(jnp.clip(j, 0, n_blk - 1) * tk, tk)
        z = jnp.dot(qs_scr[...], kt_ref[0, :, pl.ds(ks, tk)], preferred_element_type=F32)
        if masked:
            kpos = j * tk + lax.broadcasted_iota(jnp.int32, (tq, tk), 1)
            qpos = q_pos0 + lax.broadcasted_iota(jnp.int32, (tq, tk), 0)
            z = jnp.where(kpos < qpos, z, NEG)
        zb = z.astype(BF16)
        z_scr[slot] = zb
        sp_scr[slot] = _softplus(zb)

    def stage_exp(slot):
        mm = jnp.dot(sp_scr[slot], u_ref[...], preferred_element_type=F32)
        car = car_scr[...]
        rest = mm + jnp.concatenate([car] * (tk // LANES), axis=1)
        a_scr[slot] = jnp.exp(z_scr[slot] - rest.astype(BF16))
        car_scr[...] = car + jnp.broadcast_to(mm[:, 0:1], (tq, LANES))

    def stage_pv(j, slot):
        ks = pl.multiple_of(jnp.clip(j, 0, n_blk - 1) * tk, tk)
        acc_scr[...] += jnp.dot(a_scr[slot], v_ref[0, pl.ds(ks, tk), :], preferred_element_type=F32)

    def step(j, masked, slot):
        stage_qk(j, masked, slot)
        stage_exp(1 - slot)
        stage_pv(j + 2, slot)

    n_full2 = (n_full // 2) * 2
    n_diag2 = n_all - n_full2
    n_diag2 = n_diag2 + n_diag2 % 2

    def diag_body(t, c):
        j = n_full2 + n_diag2 - 1 - 2 * t
        step(j, True, 0)
        step(j - 1, True, 1)
        return c

    def full_body(t, c):
        j = n_full2 - 1 - 2 * t
        step(j, False, 0)
        step(j - 1, False, 1)
        return c

    lax.fori_loop(0, n_diag2 // 2, diag_body, 0)
    lax.fori_loop(0, n_full2 // 2, full_body, 0)
    stage_exp(1)
    stage_pv(1, 0)
    stage_pv(0, 1)
    o_ref[0] = acc_scr[...]


def _attention(q, kt, v, q_start):
    g, t, d = q.shape
    tk_total = v.shape[1]
    tq, tk = min(ATTN_TQ, t), ATTN_TK
    u = jnp.asarray((np.arange(tk)[:, None] >= np.arange(tk)[None, :]).astype(np.float32), BF16)
    kern = functools.partial(_attn_kernel, tq=tq, tk=tk, q_start=q_start)
    return pl.pallas_call(
        kern,
        grid=(g, t // tq),
        in_specs=[pl.BlockSpec((1, tq, d), lambda h, i: (h, i, 0)),
                  pl.BlockSpec((1, d, tk_total), lambda h, i: (h, 0, 0)),
                  pl.BlockSpec((1, tk_total, d), lambda h, i: (h, 0, 0)),
                  pl.BlockSpec(u.shape, lambda h, i: (0, 0))],
        out_specs=pl.BlockSpec((1, tq, d), lambda h, i: (h, i, 0)),
        out_shape=jax.ShapeDtypeStruct((g, t, d), F32),
        scratch_shapes=[pltpu.VMEM((tq, d), BF16), pltpu.VMEM((2, tq, tk), BF16),
                        pltpu.VMEM((2, tq, tk), BF16), pltpu.VMEM((2, tq, tk), BF16),
                        pltpu.VMEM((tq, d), F32), pltpu.VMEM((tq, LANES), F32)],
        compiler_params=_params(("parallel", "arbitrary")),
        name="attn",
    )(q, kt, v, u)


def _gdn_kernel(x_ref, z_ref, gt_ref, hist_ref, s0_ref, cw_ref, alog_ref, dtb_ref, ng_ref,
                o_ref, histo_ref, so_ref, xp_scr, s_scr, *, nc):
    L = CHUNK
    rows = nc * L
    step = pl.program_id(1)
    h0 = SUBLANES - (GDN_CONV - 1)

    @pl.when(step == 0)
    def _():
        xp_scr[h0:SUBLANES, :] = hist_ref[0]
        s_scr[...] = s0_ref[0]

    x = x_ref[0]
    xp_scr[SUBLANES:SUBLANES + rows, :] = x
    cw = cw_ref[...]
    conv = x * cw[GDN_CONV - 1:GDN_CONV, :]
    xp = xp_scr[...]
    for j in range(GDN_CONV - 1):
        shifted = pltpu.roll(xp, SUBLANES + rows - (h0 + j), axis=0)[:rows, :]
        conv = conv + shifted * cw[j:j + 1, :]
    new_hist = xp_scr[rows + h0:rows + SUBLANES, :]
    xp_scr[h0:SUBLANES, :] = new_hist
    histo_ref[0] = new_hist
    act = conv * _sigmoid(conv)

    gates = gt_ref[0]
    beta_t = _sigmoid(gates)
    g_t = -jnp.exp(alog_ref[...]) * _softplus(gates + dtb_ref[...])
    rr = lax.broadcasted_iota(jnp.int32, (rows, rows), 0)
    cc = lax.broadcasted_iota(jnp.int32, (rows, rows), 1)
    same_chunk = (rr // L) == (cc // L)
    gc_t = _dot_exact_lhs((cc <= rr) & same_chunk, g_t)
    egc_t = jnp.exp(gc_t)
    gc_tt = jnp.transpose(gc_t)
    row = lax.broadcasted_iota(jnp.int32, (L, L), 0)
    col = lax.broadcasted_iota(jnp.int32, (L, L), 1)
    tri = col <= row
    strict = col < row
    eye = (col == row).astype(F32)
    z_all = z_ref[0]
    ng = ng_ref[...]

    inst = [(c, h) for c in range(nc) for h in range(H_B)]

    def part(c, h, p):
        return act[c * L:(c + 1) * L, p * D_B + h * DH_B:p * D_B + (h + 1) * DH_B]

    def col(tile, c, lane0, h):
        return tile[c * L:(c + 1) * L, lane0 + h:lane0 + h + 1]

    hr = lax.broadcasted_iota(jnp.int32, (D_B, D_B), 0) // DH_B
    hc = lax.broadcasted_iota(jnp.int32, (D_B, D_B), 1) // DH_B
    head_ones = (hr == hc).astype(BF16)

    def head_sumsq(x):
        hi, lo = _split(x * x)
        return (jnp.dot(hi, head_ones, preferred_element_type=F32)
                + jnp.dot(lo, head_ones, preferred_element_type=F32))

    q_all, k_all = act[:, 0:D_B], act[:, D_B:2 * D_B]
    qn_all = q_all * (lax.rsqrt(head_sumsq(q_all) + EPS) * (DH_B ** -0.5))
    kn_all = k_all * lax.rsqrt(head_sumsq(k_all) + EPS)
    qn = [qn_all[c * L:(c + 1) * L, h * DH_B:(h + 1) * DH_B] for c, h in inst]
    kn = [kn_all[c * L:(c + 1) * L, h * DH_B:(h + 1) * DH_B] for c, h in inst]
    betas = [col(beta_t, c, G_BETA, h) for c, h in inst]
    gcs = [col(gc_t, c, G_A, h) for c, h in inst]
    egcs = [col(egc_t, c, G_A, h) for c, h in inst]
    gams = [jnp.exp(jnp.where(tri, gc - gc_tt[G_A + h:G_A + h + 1, c * L:(c + 1) * L], -jnp.inf))
            for (c, h), gc in zip(inst, gcs)]
    gls = [gc[L - 1:L, :] for gc in gcs]
    kks = [_dot_nt(k, k) for k in kn]
    nn = [-jnp.where(strict, beta * kk * gam, 0.0) for beta, kk, gam in zip(betas, kks, gams)]
    x_inv = [eye + n for n in nn]
    pw = nn
    for _ in range(5):
        pw = [_dot(p, p) for p in pw]
        x_inv = [xi + _dot(xi, p) for xi, p in zip(x_inv, pw)]
    us = [_dot(xi, beta * part(c, h, 2)) for (c, h), xi, beta in zip(inst, x_inv, betas)]
    ws = [_dot(xi, (beta * egc) * k) for xi, beta, egc, k in zip(x_inv, betas, egcs, kn)]
    qks = [jnp.where(tri, _dot_nt(q, k) * gam, 0.0) for q, k, gam in zip(qn, kn, gams)]
    qes = [q * egc for q, egc in zip(qn, egcs)]
    kdts = [jnp.transpose(k * jnp.exp(gl - gc)) for k, gl, gc in zip(kn, gls, gcs)]
    egls = [jnp.exp(gl) for gl in gls]
    kdw = [_dot(kdt, w) for kdt, w in zip(kdts, ws)]
    drive = [_dot(kdt, u) for kdt, u in zip(kdts, us)]

    states = [None] * len(inst)
    state = [s_scr[h] for h in range(H_B)]
    for c in range(nc):
        for h in range(H_B):
            states[c * H_B + h] = state[h]
        state = [state[h] * egls[c * H_B + h] - _dot(kdw[c * H_B + h], state[h]) + drive[c * H_B + h]
                 for h in range(H_B)]
    v_new = [u - _dot(w, s) for u, w, s in zip(us, ws, states)]
    outs = [_dot(qe, s) + _dot(qk, vn) for qe, s, qk, vn in zip(qes, states, qks, v_new)]
    o_ms = [jnp.mean(o * o, axis=-1, keepdims=True) for o in outs]
    o_n = [o * lax.rsqrt(ms + EPS) * ng for o, ms in zip(outs, o_ms)]
    for c in range(nc):
        zc = z_all[c * L:(c + 1) * L, :]
        o_ref[0, c * L:(c + 1) * L, :] = (
            jnp.concatenate(o_n[c * H_B:(c + 1) * H_B], axis=-1) * (zc * _sigmoid(zc)))
    for h in range(H_B):
        s_scr[h] = state[h]
    so_ref[0] = s_scr[...]


def _gdn(qkv, z, gates, hist, s0, conv_w, alog_row, dtb_row, ng_row):
    b, t, _ = qkv.shape
    nc = min(GDN_CHUNKS, t // CHUNK)
    rows = nc * CHUNK
    blk = lambda bb, c: (bb, c, 0)
    per_b3 = lambda bb, c: (bb, 0, 0)
    per_b4 = lambda bb, c: (bb, 0, 0, 0)
    fixed = lambda bb, c: (0, 0)
    return pl.pallas_call(
        functools.partial(_gdn_kernel, nc=nc),
        grid=(b, t // rows),
        in_specs=[pl.BlockSpec((1, rows, 3 * D_B), blk),
                  pl.BlockSpec((1, rows, D_B), blk),
                  pl.BlockSpec((1, rows, LANES), blk),
                  pl.BlockSpec((1, GDN_CONV - 1, 3 * D_B), per_b3),
                  pl.BlockSpec((1, H_B, DH_B, DH_B), per_b4),
                  pl.BlockSpec((GDN_CONV, 3 * D_B), fixed),
                  pl.BlockSpec((1, LANES), fixed),
                  pl.BlockSpec((1, LANES), fixed),
                  pl.BlockSpec((1, DH_B), fixed)],
        out_specs=[pl.BlockSpec((1, rows, D_B), blk),
                   pl.BlockSpec((1, GDN_CONV - 1, 3 * D_B), per_b3),
                   pl.BlockSpec((1, H_B, DH_B, DH_B), per_b4)],
        out_shape=[jax.ShapeDtypeStruct((b, t, D_B), F32),
                   jax.ShapeDtypeStruct((b, GDN_CONV - 1, 3 * D_B), F32),
                   jax.ShapeDtypeStruct((b, H_B, DH_B, DH_B), F32)],
        scratch_shapes=[pltpu.VMEM((SUBLANES + rows, 3 * D_B), F32),
                        pltpu.VMEM((H_B, DH_B, DH_B), F32)],
        compiler_params=_params(("parallel", "arbitrary")),
        name="gdn",
    )(qkv, z, gates, hist, s0, conv_w, alog_row, dtb_row, ng_row)


def _mlstm_kernel(x_ref, gt_ref, c0_ref, n0_ref, m0_ref, bi_ref, bf_ref, ng_ref,
                  o_ref, co_ref, no_ref, mo_ref, c_scr, n_scr, m_scr, *, nc):
    L = CHUNK
    rows = nc * L
    step = pl.program_id(1)

    @pl.when(step == 0)
    def _():
        c_scr[...] = c0_ref[0]
        n_scr[...] = n0_ref[0]
        m_scr[...] = m0_ref[0]

    gates = gt_ref[0]
    ig_t = gates + bi_ref[...]
    xf = gates + bf_ref[...]
    lf_t = jnp.minimum(xf, 0.0) - jnp.log(1.0 + jnp.exp(-jnp.abs(xf)))
    rr = lax.broadcasted_iota(jnp.int32, (rows, rows), 0)
    cc = lax.broadcasted_iota(jnp.int32, (rows, rows), 1)
    same_chunk = (rr // L) == (cc // L)
    bc_t = _dot_exact_lhs((cc <= rr) & same_chunk, lf_t)
    bc_tt = jnp.transpose(bc_t)
    ig_tt = jnp.transpose(ig_t)
    row = lax.broadcasted_iota(jnp.int32, (L, L), 0)
    col = lax.broadcasted_iota(jnp.int32, (L, L), 1)
    tri = col <= row
    lane = lax.broadcasted_iota(jnp.int32, (1, LANES), 1)
    m_all = m_scr[...]
    ng = ng_ref[...]

    inst = [(c, h) for c in range(nc) for h in range(H_C)]

    def sl(c, h, part):
        return x_ref[0, c * L:(c + 1) * L, part * D_C + h * DH_C:part * D_C + (h + 1) * DH_C]

    bcs = [bc_t[c * L:(c + 1) * L, G_F + h:G_F + h + 1] for c, h in inst]
    igs = [ig_t[c * L:(c + 1) * L, G_I + h:G_I + h + 1] for c, h in inst]
    dmats = [jnp.where(tri, bc - bc_tt[G_F + h:G_F + h + 1, c * L:(c + 1) * L]
                       + ig_tt[G_I + h:G_I + h + 1, c * L:(c + 1) * L], -jnp.inf)
             for (c, h), bc in zip(inst, bcs)]
    dmaxs = [jnp.max(d, axis=-1, keepdims=True) for d in dmats]
    ks = [sl(c, h, 1) * (DH_C ** -0.5) for c, h in inst]
    qk_raw = [_dot_nt(sl(c, h, 0), k) for (c, h), k in zip(inst, ks)]

    gs, ms = [None] * len(inst), [None] * len(inst)
    for h in range(H_C):
        m0 = m_all[:, h:h + 1]
        for c in range(nc):
            i = c * H_C + h
            gs[i] = bcs[i] + m0
            ms[i] = jnp.maximum(gs[i], dmaxs[i])
            m0 = ms[i][L - 1:L, :]
        m_all = jnp.where(lane == h, m0, m_all)
    m_last = [m[L - 1:L, :] for m in ms]
    decays = [jnp.exp(g[L - 1:L, :] - ml) for g, ml in zip(gs, m_last)]
    kds = [k * jnp.exp(bc[L - 1:L, :] - bc + ig - ml) for k, bc, ig, ml in zip(ks, bcs, igs, m_last)]
    kvs = [_dot(jnp.transpose(kd), sl(c, h, 2)) for (c, h), kd in zip(inst, kds)]
    ksums = [jnp.sum(kd, axis=0, keepdims=True) for kd in kds]

    cms, nrows = [None] * len(inst), [None] * len(inst)
    for h in range(H_C):
        cm, nrow = c_scr[h], n_scr[h:h + 1, :]
        for c in range(nc):
            i = c * H_C + h
            cms[i], nrows[i] = cm, nrow
            cm = decays[i] * cm + kvs[i]
            nrow = decays[i] * nrow + ksums[i]
        c_scr[h] = cm
        n_scr[h:h + 1, :] = nrow

    inters = [jnp.exp(g - m) for g, m in zip(gs, ms)]
    qks = [r * jnp.exp(d - m) for r, d, m in zip(qk_raw, dmats, ms)]
    nums = [it * _dot(sl(c, h, 0), cm) + _dot(qk, sl(c, h, 2))
            for (c, h), it, cm, qk in zip(inst, inters, cms, qks)]
    qns = [jnp.sum(sl(c, h, 0) * nrow, axis=-1, keepdims=True) for (c, h), nrow in zip(inst, nrows)]
    qksums = [jnp.sum(qk, axis=-1, keepdims=True) for qk in qks]
    dens = [it * qn + s for it, qn, s in zip(inters, qns, qksums)]
    hhs = [num / jnp.maximum(jnp.abs(den), jnp.exp(-m)) for num, den, m in zip(nums, dens, ms)]
    mss = [jnp.mean(hh * hh, axis=-1, keepdims=True) for hh in hhs]
    for (c, h), hh, msq in zip(inst, hhs, mss):
        o_ref[0, c * L:(c + 1) * L, h * DH_C:(h + 1) * DH_C] = (
            hh * lax.rsqrt(msq + EPS) * ng * _sigmoid(sl(c, h, 3)))
    m_scr[...] = m_all
    co_ref[0] = c_scr[...]
    no_ref[0] = n_scr[...]
    mo_ref[0] = m_all


def _mlstm(x, gates, c0, n0, m0, bi_row, bf_row, ng_row):
    b, t, _ = x.shape
    nc = min(MLSTM_CHUNKS, t // CHUNK)
    rows = nc * CHUNK
    blk = lambda bb, c: (bb, c, 0)
    per_b3 = lambda bb, c: (bb, 0, 0)
    per_b4 = lambda bb, c: (bb, 0, 0, 0)
    fixed = lambda bb, c: (0, 0)
    return pl.pallas_call(
        functools.partial(_mlstm_kernel, nc=nc),
        grid=(b, t // rows),
        in_specs=[pl.BlockSpec((1, rows, 4 * D_C), blk),
                  pl.BlockSpec((1, rows, LANES), blk),
                  pl.BlockSpec((1, H_C, DH_C, DH_C), per_b4),
                  pl.BlockSpec((1, H_C, DH_C), per_b3),
                  pl.BlockSpec((1, 1, LANES), per_b3),
                  pl.BlockSpec((1, LANES), fixed),
                  pl.BlockSpec((1, LANES), fixed),
                  pl.BlockSpec((1, DH_C), fixed)],
        out_specs=[pl.BlockSpec((1, rows, D_C), blk),
                   pl.BlockSpec((1, H_C, DH_C, DH_C), per_b4),
                   pl.BlockSpec((1, H_C, DH_C), per_b3),
                   pl.BlockSpec((1, 1, LANES), per_b3)],
        out_shape=[jax.ShapeDtypeStruct((b, t, D_C), F32),
                   jax.ShapeDtypeStruct((b, H_C, DH_C, DH_C), F32),
                   jax.ShapeDtypeStruct((b, H_C, DH_C), F32),
                   jax.ShapeDtypeStruct((b, 1, LANES), F32)],
        scratch_shapes=[pltpu.VMEM((H_C, DH_C, DH_C), F32),
                        pltpu.VMEM((H_C, DH_C), F32),
                        pltpu.VMEM((1, LANES), F32)],
        compiler_params=_params(("parallel", "arbitrary")),
        name="mlstm",
    )(x, gates, c0, n0, m0, bi_row, bf_row, ng_row)


def _outproj_kernel(x_ref, a_ref, b_ref, c_ref, w_ref, g_ref, o_ref):
    mix = (_dot(a_ref[...], w_ref[0:D_A, :])
           + _dot(b_ref[...], w_ref[D_A:D_A + D_B, :])
           + _dot(c_ref[...], w_ref[D_A + D_B:, :]))
    o_ref[...] = x_ref[...] + _rms(mix, g_ref[...])


def _outproj(x, oa, ob, oc, w_out, g):
    n = x.shape[0]
    r = min(ROW_TILE, n)
    row = lambda i: (i, 0)
    fixed = lambda i: (0, 0)
    return pl.pallas_call(
        _outproj_kernel,
        grid=(n // r,),
        in_specs=[pl.BlockSpec((r, D_MODEL), row),
                  pl.BlockSpec((r, D_A), row),
                  pl.BlockSpec((r, D_B), row),
                  pl.BlockSpec((r, D_C), row),
                  pl.BlockSpec(w_out.shape, fixed),
                  pl.BlockSpec((1, D_MODEL), fixed)],
        out_specs=pl.BlockSpec((r, D_MODEL), row),
        out_shape=jax.ShapeDtypeStruct((n, D_MODEL), F32),
        compiler_params=_params(("parallel",)),
        name="outproj",
    )(x, oa, ob, oc, w_out, g)


def _ffn_kernel(x_ref, hist_ref, g1_ref, wu_ref, cw_ref, wd_ref, g2_ref,
                o_ref, histo_ref, gp_scr, hist_scr, acc_scr, *, rows):
    t = pl.program_id(1)
    h0 = SUBLANES - (FFN_CONV - 1)

    @pl.when(t == 0)
    def _():
        hist_scr[h0:SUBLANES, :] = hist_ref[0]

    x = x_ref[0]
    h = _rms(x, g1_ref[...]).astype(BF16)
    acc_scr[...] = jnp.zeros_like(acc_scr)
    for ci in range(D_FF // FF_CHUNK):
        lo, hi = ci * FF_CHUNK, (ci + 1) * FF_CHUNK
        gate = jnp.dot(h, wu_ref[:, lo:hi], preferred_element_type=F32)
        up = jnp.dot(h, wu_ref[:, D_FF + lo:D_FF + hi], preferred_element_type=F32)
        gp_scr[h0:SUBLANES, :] = hist_scr[h0:SUBLANES, lo:hi]
        gp_scr[SUBLANES:SUBLANES + rows, :] = gate
        cw = cw_ref[:, lo:hi]
        conv = gate * cw[FFN_CONV - 1:FFN_CONV, :]
        for j in range(FFN_CONV - 1):
            conv = conv + gp_scr[h0 + j:h0 + j + rows, :] * cw[j:j + 1, :]
        hist_scr[h0:SUBLANES, lo:hi] = gp_scr[rows + h0:rows + SUBLANES, :]
        act = jax.nn.gelu(conv, approximate=True) * up
        acc_scr[...] += jnp.dot(act.astype(BF16), wd_ref[lo:hi, :], preferred_element_type=F32)
    histo_ref[0] = hist_scr[h0:SUBLANES, :]
    o_ref[0] = x + _rms(acc_scr[...], g2_ref[...])


def _ffn(x, hist, g1, w_up, conv_w, w_down, g2):
    b, t, _ = x.shape
    r = min(ROW_TILE, t)
    blk = lambda bb, i: (bb, i, 0)
    per_b = lambda bb, i: (bb, 0, 0)
    fixed = lambda bb, i: (0, 0)
    kern = functools.partial(_ffn_kernel, rows=r)
    return pl.pallas_call(
        kern,
        grid=(b, t // r),
        in_specs=[pl.BlockSpec((1, r, D_MODEL), blk),
                  pl.BlockSpec((1, FFN_CONV - 1, D_FF), per_b),
                  pl.BlockSpec((1, D_MODEL), fixed),
                  pl.BlockSpec(w_up.shape, fixed),
                  pl.BlockSpec((FFN_CONV, D_FF), fixed),
                  pl.BlockSpec(w_down.shape, fixed),
                  pl.BlockSpec((1, D_MODEL), fixed)],
        out_specs=[pl.BlockSpec((1, r, D_MODEL), blk),
                   pl.BlockSpec((1, FFN_CONV - 1, D_FF), per_b)],
        out_shape=[jax.ShapeDtypeStruct((b, t, D_MODEL), F32),
                   jax.ShapeDtypeStruct((b, FFN_CONV - 1, D_FF), F32)],
        scratch_shapes=[pltpu.VMEM((SUBLANES + r, FF_CHUNK), F32),
                        pltpu.VMEM((SUBLANES, D_FF), F32),
                        pltpu.VMEM((r, D_MODEL), F32)],
        compiler_params=_params(("parallel", "arbitrary")),
        name="ffn",
    )(x, hist, g1, w_up, conv_w, w_down, g2)


def _lane_row(vals, offset):
    return jnp.zeros((1, LANES), F32).at[0, offset:offset + vals.shape[0]].set(vals.astype(F32))


def _prep_weights(w_in, w_out, ffn_w_up, ffn_w_down):
    c_gate0 = 3 * D_A + 4 * D_B
    c_c0 = c_gate0 + 2 * H_B
    c_gate1 = c_c0 + 4 * D_C
    wb = w_in.astype(BF16)
    w_main = jnp.concatenate([wb[:, :c_gate0], wb[:, c_c0:c_gate1]], axis=1)
    n_gate = 2 * H_B + 2 * H_C
    w_gate = jnp.concatenate([wb[:, c_gate0:c_c0], wb[:, c_gate1:],
                              jnp.zeros((D_MODEL, LANES - n_gate), BF16)], axis=1)
    return w_main, w_gate, w_out.astype(BF16), ffn_w_up.astype(BF16), ffn_w_down.astype(BF16)


def _to_heads(x, b, t, h):
    return x.reshape(b, t, h, -1).transpose(0, 2, 1, 3)


def _layer(x, kv_k, kv_v, gdn_hist, gdn_s, m_c, m_n, m_m, ffn_hist, lw):
    (g_mix_pre, g_mix_post, g_ffn_pre, g_ffn_post, w_main, w_gate, gdn_conv_w, alog_row, dtb_row,
     gdn_ng, bi_row, bf_row, mlstm_ng, w_out, w_up, ffn_conv_w, w_down) = lw
    b, t, _ = x.shape
    n = b * t
    xf = x.reshape(n, D_MODEL)
    qkv_a, qkv_b, z_b, qkvo_c, gates = _inproj(xf, g_mix_pre, w_main, w_gate)

    k_new = _to_heads(qkv_a[:, D_A:2 * D_A], b, t, H_A)
    v_new = _to_heads(qkv_a[:, 2 * D_A:], b, t, H_A)
    q_h = _to_heads(qkv_a[:, :D_A], b, t, H_A).astype(BF16)
    past = 0 if kv_k is None else kv_k.shape[2]
    tk_total = -(-(past + t) // ATTN_TK) * ATTN_TK
    parts_k = [k_new.astype(BF16)] if kv_k is None else [kv_k.astype(BF16), k_new.astype(BF16)]
    parts_v = [v_new.astype(BF16)] if kv_v is None else [kv_v.astype(BF16), v_new.astype(BF16)]
    pad = tk_total - (past + t)
    if pad > 0:
        zpad = jnp.zeros((b, H_A, pad, DH_A), BF16)
        parts_k.append(zpad)
        parts_v.append(zpad)
    k_all = jnp.concatenate(parts_k, axis=2) if len(parts_k) > 1 else parts_k[0]
    v_all = jnp.concatenate(parts_v, axis=2) if len(parts_v) > 1 else parts_v[0]
    g = b * H_A
    kt_all = k_all.transpose(0, 1, 3, 2)
    o_a = _attention(q_h.reshape(g, t, DH_A), kt_all.reshape(g, DH_A, tk_total),
                     v_all.reshape(g, tk_total, DH_A), past)
    o_a = o_a.reshape(b, H_A, t, DH_A).transpose(0, 2, 1, 3).reshape(n, D_A)

    o_b, gdn_hist_new, s_new = _gdn(qkv_b.reshape(b, t, 3 * D_B), z_b.reshape(b, t, D_B),
                                    gates.reshape(b, t, LANES), gdn_hist, gdn_s, gdn_conv_w,
                                    alog_row, dtb_row, gdn_ng)

    o_c, c_new, n_new, m_new = _mlstm(qkvo_c.reshape(b, t, 4 * D_C), gates.reshape(b, t, LANES),
                                      m_c, m_n, m_m, bi_row, bf_row, mlstm_ng)

    x1 = _outproj(xf, o_a, o_b.reshape(n, D_B), o_c.reshape(n, D_C), w_out, g_mix_post)
    x2, ffn_hist_new = _ffn(x1.reshape(b, t, D_MODEL), ffn_hist, g_ffn_pre, w_up, ffn_conv_w,
                            w_down, g_ffn_post)
    return (x2, k_new, v_new, gdn_hist_new, s_new, c_new, n_new, m_new[:, 0, :H_C], ffn_hist_new)


def kernel(x_prompt, x_sample, cache_sb_k, cache_sb_v, state_gdn_conv, state_gdn_s, state_mlstm_c, state_mlstm_n, state_mlstm_m, state_ffn_conv, g_mix_pre, g_mix_post, g_ffn_pre, g_ffn_post, w_in, gdn_conv_w, gdn_a_log, gdn_dt_bias, gdn_norm_g, mlstm_b_i, mlstm_b_f, mlstm_norm_g, w_out, ffn_w_up, ffn_conv_w, ffn_w_down):
    depth = w_in.shape[0]
    bp = x_prompt.shape[0]
    bs = x_sample.shape[0]
    zero_gdn_hist = jnp.zeros((bp, GDN_CONV - 1, 3 * D_B), F32)
    zero_s = jnp.zeros((bp, H_B, DH_B, DH_B), F32)
    zero_c = jnp.zeros((bp, H_C, DH_C, DH_C), F32)
    zero_n = jnp.zeros((bp, H_C, DH_C), F32)
    m_init = jnp.full((bp, 1, LANES), NEG, F32)
    zero_ffn_hist = jnp.zeros((bp, FFN_CONV - 1, D_FF), F32)

    xp, xs = x_prompt, x_sample
    new_p, new_s = [], []
    for l in range(depth):
        w_main, w_gate, w_o, w_u, w_d = _prep_weights(w_in[l], w_out[l], ffn_w_up[l], ffn_w_down[l])
        lw = (g_mix_pre[l][None], g_mix_post[l][None], g_ffn_pre[l][None], g_ffn_post[l][None],
              w_main, w_gate, gdn_conv_w[l], _lane_row(gdn_a_log[l], G_A),
              _lane_row(gdn_dt_bias[l], G_A), gdn_norm_g[l][None],
              _lane_row(mlstm_b_i[l], G_I), _lane_row(mlstm_b_f[l], G_F), mlstm_norm_g[l][None],
              w_o, w_u, ffn_conv_w[l], w_d)
        xp, *st_p = _layer(xp, None, None, zero_gdn_hist, zero_s, zero_c, zero_n, m_init,
                           zero_ffn_hist, lw)
        m_s = jnp.zeros((bs, 1, LANES), F32).at[:, 0, :H_C].set(state_mlstm_m[l])
        xs, *st_s = _layer(xs, cache_sb_k[l], cache_sb_v[l], state_gdn_conv[l], state_gdn_s[l],
                           state_mlstm_c[l], state_mlstm_n[l], m_s, state_ffn_conv[l], lw)
        new_p.append(st_p)
        new_s.append(st_s)
    outs_p = [jnp.stack(a) for a in zip(*new_p)]
    outs_s = [jnp.stack(a) for a in zip(*new_s)]
    return (xp, xs, *outs_p, *outs_s)
```

```python
import functools

import jax
import jax.numpy as jnp
import numpy as np
from jax import lax
from jax.experimental import pallas as pl
from jax.experimental.pallas import tpu as pltpu

F32 = jnp.float32
BF16 = jnp.bfloat16

D_MODEL = 1024
CHUNK = 64
H_A, DH_A = 4, 64
H_B, DH_B = 4, 64
H_C, DH_C = 4, 128
D_A, D_B, D_C = H_A * DH_A, H_B * DH_B, H_C * DH_C
GDN_CONV = 4
D_FF = 2816
FFN_CONV = 3
EPS = 1e-6
NEG = -1e30

LANES = 128
SUBLANES = 8
ROW_TILE = 512
ATTN_TQ = 1024
ATTN_TK = 256
FF_CHUNK = 2816
GDN_CHUNKS = 4
MLSTM_CHUNKS = 4
VMEM_LIMIT = 56 * 1024 * 1024

G_BETA, G_A, G_I, G_F = 0, 4, 8, 12


def _params(sem):
    return pltpu.CompilerParams(dimension_semantics=sem, vmem_limit_bytes=VMEM_LIMIT)


def _dot(a, b):
    return jnp.dot(a.astype(BF16), b.astype(BF16), preferred_element_type=F32)


def _dot_nt(a, b):
    return lax.dot_general(a.astype(BF16), b.astype(BF16), (((1,), (1,)), ((), ())),
                           preferred_element_type=F32)


def _split(a):
    hi = a.astype(BF16)
    lo = (a - hi.astype(F32)).astype(BF16)
    return hi, lo


def _dot3(a, b):
    ah, al = _split(a)
    bh, bl = _split(b)
    return (jnp.dot(ah, bh, preferred_element_type=F32)
            + jnp.dot(ah, bl, preferred_element_type=F32)
            + jnp.dot(al, bh, preferred_element_type=F32))


def _dot_exact_lhs(a01, b):
    a = a01.astype(BF16)
    b0 = b.astype(BF16)
    r1 = b - b0.astype(F32)
    b1 = r1.astype(BF16)
    b2 = (r1 - b1.astype(F32)).astype(BF16)
    return (jnp.dot(a, b0, preferred_element_type=F32)
            + jnp.dot(a, b1, preferred_element_type=F32)
            + jnp.dot(a, b2, preferred_element_type=F32))


def _sigmoid(x):
    return 1.0 / (1.0 + jnp.exp(-x))


def _softplus(x):
    return jnp.maximum(x, 0.0) + jnp.log(1.0 + jnp.exp(-jnp.abs(x)))


def _rms(x, g):
    return x * lax.rsqrt(jnp.mean(x * x, axis=-1, keepdims=True) + EPS) * g


def _inproj_kernel(x_ref, g_ref, w_ref, wg_ref, q_ref, kt_ref, v_ref, kf_ref, vf_ref,
                   b_ref, z_ref, c_ref, gt_ref):
    h = _rms(x_ref[...], g_ref[...]).astype(BF16)
    qkv = jnp.dot(h, w_ref[:, 0:3 * D_A], preferred_element_type=F32)
    for hh in range(H_A):
        lo = hh * DH_A
        q_ref[hh] = qkv[:, lo:lo + DH_A].astype(BF16)
        kf_ref[hh] = qkv[:, D_A + lo:D_A + lo + DH_A]
        v_h = qkv[:, 2 * D_A + lo:2 * D_A + lo + DH_A]
        vf_ref[hh] = v_h
        v_ref[hh] = v_h.astype(BF16)
    for pair in range(H_A * DH_A // LANES):
        kt_pair = jnp.transpose(qkv[:, D_A + pair * LANES:D_A + (pair + 1) * LANES])
        for sub in range(LANES // DH_A):
            kt_ref[pair * (LANES // DH_A) + sub] = kt_pair[sub * DH_A:(sub + 1) * DH_A, :].astype(BF16)
    b_ref[...] = jnp.dot(h, w_ref[:, 768:1536], preferred_element_type=F32)
    z_ref[...] = jnp.dot(h, w_ref[:, 1536:1792], preferred_element_type=F32)
    c_ref[...] = jnp.dot(h, w_ref[:, 1792:3840], preferred_element_type=F32)
    gt_ref[...] = jnp.dot(h, wg_ref[...], preferred_element_type=F32)


def _inproj(x, g, w_main, w_gate):
    n = x.shape[0]
    r = min(ROW_TILE, n)
    assert n % r == 0
    widths = (3 * D_B, D_B, 4 * D_C, LANES)
    row = lambda i: (i, 0)
    fixed = lambda i: (0, 0)
    head_rows = lambda i: (0, i, 0)
    head_cols = lambda i: (0, 0, i)
    hm_spec = pl.BlockSpec((H_A, r, DH_A), head_rows)
    hm = lambda dt: jax.ShapeDtypeStruct((H_A, n, DH_A), dt)
    return pl.pallas_call(
        _inproj_kernel,
        grid=(n // r,),
        in_specs=[pl.BlockSpec((r, D_MODEL), row),
                  pl.BlockSpec((1, D_MODEL), fixed),
                  pl.BlockSpec(w_main.shape, fixed),
                  pl.BlockSpec(w_gate.shape, fixed)],
        out_specs=[hm_spec, pl.BlockSpec((H_A, DH_A, r), head_cols), hm_spec, hm_spec, hm_spec]
                  + [pl.BlockSpec((r, w), row) for w in widths],
        out_shape=[hm(BF16), jax.ShapeDtypeStruct((H_A, DH_A, n), BF16), hm(BF16), hm(F32), hm(F32)]
                  + [jax.ShapeDtypeStruct((n, w), F32) for w in widths],
        compiler_params=_params(("parallel",)),
        name="inproj",
    )(x, g, w_main, w_gate)


def _attn_kernel(q_ref, kt_ref, v_ref, u_ref, o_ref, qs_scr, z_scr, sp_scr, a_scr, acc_scr, car_scr,
                 *, tq, tk, q_start):
    i = pl.program_id(1)
    q_pos0 = q_start + i * tq
    n_all = (q_pos0 + tq - 1 + tk - 1) // tk
    n_full = q_pos0 // tk
    n_blk = v_ref.shape[1] // tk
    qs_scr[...] = q_ref[0] * jnp.asarray(DH_A ** -0.5, BF16)
    acc_scr[...] = jnp.zeros_like(acc_scr)
    car_scr[...] = jnp.zeros_like(car_scr)
    z_scr[1] = jnp.full((tq, tk), NEG, BF16)
    sp_scr[1] = jnp.zeros((tq, tk), BF16)
    a_scr[0] = jnp.zeros((tq, tk), BF16)

    def stage_qk(j, masked, slot):
        ks = pl.multiple_of(jnp.clip(j, 0, n_blk - 1) * tk, tk)
        z = jnp.dot(qs_scr[...], kt_ref[0, :, pl.ds(ks, tk)], preferred_element_type=F32)
        if masked:
            kpos = j * tk + lax.broadcasted_iota(jnp.int32, (tq, tk), 1)
            qpos = q_pos0 + lax.broadcasted_iota(jnp.int32, (tq, tk), 0)
            z = jnp.where(kpos < qpos, z, NEG)
        zb = z.astype(BF16)
        z_scr[slot] = zb
        sp_scr[slot] = _softplus(zb)

    def stage_exp(slot):
        mm = jnp.dot(sp_scr[slot], u_ref[...], preferred_element_type=F32)
        car = car_scr[...]
        rest = mm + jnp.concatenate([car] * (tk // LANES), axis=1)
        a_scr[slot] = jnp.exp(z_scr[slot] - rest.astype(BF16))
        car_scr[...] = car + jnp.broadcast_to(mm[:, 0:1], (tq, LANES))

    def stage_pv(j, slot):
        ks = pl.multiple_of(jnp.clip(j, 0, n_blk - 1) * tk, tk)
        acc_scr[...] += jnp.dot(a_scr[slot], v_ref[0, pl.ds(ks, tk), :], preferred_element_type=F32)

    def step(j, masked, slot):
        stage_qk(j, masked, slot)
        stage_exp(1 - slot)
        stage_pv(j + 2, slot)

    n_full2 = (n_full // 2) * 2
    n_diag2 = n_all - n_full2
    n_diag2 = n_diag2 + n_diag2 % 2

    def diag_body(t, c):
        j = n_full2 + n_diag2 - 1 - 2 * t
        step(j, True, 0)
        step(j - 1, True, 1)
        return c

    def full_body(t, c):
        j = n_full2 - 1 - 2 * t
        step(j, False, 0)
        step(j - 1, False, 1)
        return c

    lax.fori_loop(0, n_diag2 // 2, diag_body, 0)
    lax.fori_loop(0, n_full2 // 2, full_body, 0)
    stage_exp(1)
    stage_pv(1, 0)
    stage_pv(0, 1)
    o_ref[0] = acc_scr[...]


def _attention(q, kt, v, q_start):
    g, t, d = q.shape
    tk_total = v.shape[1]
    tq, tk = min(ATTN_TQ, t), ATTN_TK
    u = jnp.asarray((np.arange(tk)[:, None] >= np.arange(tk)[None, :]).astype(np.float32), BF16)
    kern = functools.partial(_attn_kernel, tq=tq, tk=tk, q_start=q_start)
    return pl.pallas_call(
        kern,
        grid=(g, t // tq),
        in_specs=[pl.BlockSpec((1, tq, d), lambda h, i: (h, i, 0)),
                  pl.BlockSpec((1, d, tk_total), lambda h, i: (h, 0, 0)),
                  pl.BlockSpec((1, tk_total, d), lambda h, i: (h, 0, 0)),
                  pl.BlockSpec(u.shape, lambda h, i: (0, 0))],
        out_specs=pl.BlockSpec((1, tq, d), lambda h, i: (h, i, 0)),
        out_shape=jax.ShapeDtypeStruct((g, t, d), F32),
        scratch_shapes=[pltpu.VMEM((tq, d), BF16), pltpu.VMEM((2, tq, tk), BF16),
                        pltpu.VMEM((2, tq, tk), BF16), pltpu.VMEM((2, tq, tk), BF16),
                        pltpu.VMEM((tq, d), F32), pltpu.VMEM((tq, LANES), F32)],
        compiler_params=_params(("parallel", "arbitrary")),
        name="attn",
    )(q, kt, v, u)


def _gdn_kernel(x_ref, z_ref, gt_ref, hist_ref, s0_ref, cw_ref, alog_ref, dtb_ref, ng_ref,
                o_ref, histo_ref, so_ref, xp_scr, s_scr, *, nc):
    L = CHUNK
    rows = nc * L
    step = pl.program_id(1)
    h0 = SUBLANES - (GDN_CONV - 1)

    @pl.when(step == 0)
    def _():
        xp_scr[h0:SUBLANES, :] = hist_ref[0]
        s_scr[...] = s0_ref[0]

    x = x_ref[0]
    xp_scr[SUBLANES:SUBLANES + rows, :] = x
    cw = cw_ref[...]
    conv = x * cw[GDN_CONV - 1:GDN_CONV, :]
    xp = xp_scr[...]
    for j in range(GDN_CONV - 1):
        shifted = pltpu.roll(xp, SUBLANES + rows - (h0 + j), axis=0)[:rows, :]
        conv = conv + shifted * cw[j:j + 1, :]
    new_hist = xp_scr[rows + h0:rows + SUBLANES, :]
    xp_scr[h0:SUBLANES, :] = new_hist
    histo_ref[0] = new_hist
    act = conv * _sigmoid(conv)

    gates = gt_ref[0]
    beta_t = _sigmoid(gates)
    g_t = -jnp.exp(alog_ref[...]) * _softplus(gates + dtb_ref[...])
    rr = lax.broadcasted_iota(jnp.int32, (rows, rows), 0)
    cc = lax.broadcasted_iota(jnp.int32, (rows, rows), 1)
    same_chunk = (rr // L) == (cc // L)
    gc_t = _dot_exact_lhs((cc <= rr) & same_chunk, g_t)
    egc_t = jnp.exp(gc_t)
    gc_tt = jnp.transpose(gc_t)
    row = lax.broadcasted_iota(jnp.int32, (L, L), 0)
    col = lax.broadcasted_iota(jnp.int32, (L, L), 1)
    tri = col <= row
    strict = col < row
    eye = (col == row).astype(F32)
    z_all = z_ref[0]
    ng = ng_ref[...]

    inst = [(c, h) for c in range(nc) for h in range(H_B)]

    def part(c, h, p):
        return act[c * L:(c + 1) * L, p * D_B + h * DH_B:p * D_B + (h + 1) * DH_B]

    def col(tile, c, lane0, h):
        return tile[c * L:(c + 1) * L, lane0 + h:lane0 + h + 1]

    hr = lax.broadcasted_iota(jnp.int32, (D_B, D_B), 0) // DH_B
    hc = lax.broadcasted_iota(jnp.int32, (D_B, D_B), 1) // DH_B
    head_ones = (hr == hc).astype(BF16)

    def head_sumsq(x):
        hi, lo = _split(x * x)
        return (jnp.dot(hi, head_ones, preferred_element_type=F32)
                + jnp.dot(lo, head_ones, preferred_element_type=F32))

    q_all, k_all = act[:, 0:D_B], act[:, D_B:2 * D_B]
    qn_all = q_all * (lax.rsqrt(head_sumsq(q_all) + EPS) * (DH_B ** -0.5))
    kn_all = k_all * lax.rsqrt(head_sumsq(k_all) + EPS)
    qn = [qn_all[c * L:(c + 1) * L, h * DH_B:(h + 1) * DH_B] for c, h in inst]
    kn = [kn_all[c * L:(c + 1) * L, h * DH_B:(h + 1) * DH_B] for c, h in inst]
    betas = [col(beta_t, c, G_BETA, h) for c, h in inst]
    gcs = [col(gc_t, c, G_A, h) for c, h in inst]
    egcs = [col(egc_t, c, G_A, h) for c, h in inst]
    gams = [jnp.exp(jnp.where(tri, gc - gc_tt[G_A + h:G_A + h + 1, c * L:(c + 1) * L], -jnp.inf))
            for (c, h), gc in zip(inst, gcs)]
    gls = [gc[L - 1:L, :] for gc in gcs]
    kks = [_dot_nt(k, k) for k in kn]
    nn = [-jnp.where(strict, beta * kk * gam, 0.0) for beta, kk, gam in zip(betas, kks, gams)]
    x_inv = [eye + n for n in nn]
    pw = nn
    for _ in range(5):
        pw = [_dot(p, p) for p in pw]
        x_inv = [xi + _dot(xi, p) for xi, p in zip(x_inv, pw)]
    us = [_dot(xi, beta * part(c, h, 2)) for (c, h), xi, beta in zip(inst, x_inv, betas)]
    ws = [_dot(xi, (beta * egc) * k) for xi, beta, egc, k in zip(x_inv, betas, egcs, kn)]
    qks = [jnp.where(tri, _dot_nt(q, k) * gam, 0.0) for q, k, gam in zip(qn, kn, gams)]
    qes = [q * egc for q, egc in zip(qn, egcs)]
    kdts = [jnp.transpose(k * jnp.exp(gl - gc)) for k, gl, gc in zip(kn, gls, gcs)]
    egls = [jnp.exp(gl) for gl in gls]
    kdw = [_dot(kdt, w) for kdt, w in zip(kdts, ws)]
    drive = [_dot(kdt, u) for kdt, u in zip(kdts, us)]

    states = [None] * len(inst)
    state = [s_scr[h] for h in range(H_B)]
    for c in range(nc):
        for h in range(H_B):
            states[c * H_B + h] = state[h]
        state = [state[h] * egls[c * H_B + h] - _dot(kdw[c * H_B + h], state[h]) + drive[c * H_B + h]
                 for h in range(H_B)]
    v_new = [u - _dot(w, s) for u, w, s in zip(us, ws, states)]
    outs = [_dot(qe, s) + _dot(qk, vn) for qe, s, qk, vn in zip(qes, states, qks, v_new)]
    o_ms = [jnp.mean(o * o, axis=-1, keepdims=True) for o in outs]
    o_n = [o * lax.rsqrt(ms + EPS) * ng for o, ms in zip(outs, o_ms)]
    for c in range(nc):
        zc = z_all[c * L:(c + 1) * L, :]
        o_ref[0, c * L:(c + 1) * L, :] = (
            jnp.concatenate(o_n[c * H_B:(c + 1) * H_B], axis=-1) * (zc * _sigmoid(zc)))
    for h in range(H_B):
        s_scr[h] = state[h]
    so_ref[0] = s_scr[...]


def _gdn(qkv, z, gates, hist, s0, conv_w, alog_row, dtb_row, ng_row):
    b, t, _ = qkv.shape
    nc = min(GDN_CHUNKS, t // CHUNK)
    rows = nc * CHUNK
    blk = lambda bb, c: (bb, c, 0)
    per_b3 = lambda bb, c: (bb, 0, 0)
    per_b4 = lambda bb, c: (bb, 0, 0, 0)
    fixed = lambda bb, c: (0, 0)
    return pl.pallas_call(
        functools.partial(_gdn_kernel, nc=nc),
        grid=(b, t // rows),
        in_specs=[pl.BlockSpec((1, rows, 3 * D_B), blk),
                  pl.BlockSpec((1, rows, D_B), blk),
                  pl.BlockSpec((1, rows, LANES), blk),
                  pl.BlockSpec((1, GDN_CONV - 1, 3 * D_B), per_b3),
                  pl.BlockSpec((1, H_B, DH_B, DH_B), per_b4),
                  pl.BlockSpec((GDN_CONV, 3 * D_B), fixed),
                  pl.BlockSpec((1, LANES), fixed),
                  pl.BlockSpec((1, LANES), fixed),
                  pl.BlockSpec((1, DH_B), fixed)],
        out_specs=[pl.BlockSpec((1, rows, D_B), blk),
                   pl.BlockSpec((1, GDN_CONV - 1, 3 * D_B), per_b3),
                   pl.BlockSpec((1, H_B, DH_B, DH_B), per_b4)],
        out_shape=[jax.ShapeDtypeStruct((b, t, D_B), F32),
                   jax.ShapeDtypeStruct((b, GDN_CONV - 1, 3 * D_B), F32),
                   jax.ShapeDtypeStruct((b, H_B, DH_B, DH_B), F32)],
        scratch_shapes=[pltpu.VMEM((SUBLANES + rows, 3 * D_B), F32),
                        pltpu.VMEM((H_B, DH_B, DH_B), F32)],
        compiler_params=_params(("parallel", "arbitrary")),
        name="gdn",
    )(qkv, z, gates, hist, s0, conv_w, alog_row, dtb_row, ng_row)


def _mlstm_kernel(x_ref, gt_ref, c0_ref, n0_ref, m0_ref, bi_ref, bf_ref, ng_ref,
                  o_ref, co_ref, no_ref, mo_ref, c_scr, n_scr, m_scr, *, nc):
    L = CHUNK
    rows = nc * L
    step = pl.program_id(1)

    @pl.when(step == 0)
    def _():
        c_scr[...] = c0_ref[0]
        n_scr[...] = n0_ref[0]
        m_scr[...] = m0_ref[0]

    gates = gt_ref[0]
    ig_t = gates + bi_ref[...]
    xf = gates + bf_ref[...]
    lf_t = jnp.minimum(xf, 0.0) - jnp.log(1.0 + jnp.exp(-jnp.abs(xf)))
    rr = lax.broadcasted_iota(jnp.int32, (rows, rows), 0)
    cc = lax.broadcasted_iota(jnp.int32, (rows, rows), 1)
    same_chunk = (rr // L) == (cc // L)
    bc_t = _dot_exact_lhs((cc <= rr) & same_chunk, lf_t)
    bc_tt = jnp.transpose(bc_t)
    ig_tt = jnp.transpose(ig_t)
    row = lax.broadcasted_iota(jnp.int32, (L, L), 0)
    col = lax.broadcasted_iota(jnp.int32, (L, L), 1)
    tri = col <= row
    lane = lax.broadcasted_iota(jnp.int32, (1, LANES), 1)
    m_all = m_scr[...]
    ng = ng_ref[...]

    inst = [(c, h) for c in range(nc) for h in range(H_C)]

    def sl(c, h, part):
        return x_ref[0, c * L:(c + 1) * L, part * D_C + h * DH_C:part * D_C + (h + 1) * DH_C]

    bcs = [bc_t[c * L:(c + 1) * L, G_F + h:G_F + h + 1] for c, h in inst]
    igs = [ig_t[c * L:(c + 1) * L, G_I + h:G_I + h + 1] for c, h in inst]
    dmats = [jnp.where(tri, bc - bc_tt[G_F + h:G_F + h + 1, c * L:(c + 1) * L]
                       + ig_tt[G_I + h:G_I + h + 1, c * L:(c + 1) * L], -jnp.inf)
             for (c, h), bc in zip(inst, bcs)]
    dmaxs = [jnp.max(d, axis=-1, keepdims=True) for d in dmats]
    ks = [sl(c, h, 1) * (DH_C ** -0.5) for c, h in inst]
    qk_raw = [_dot_nt(sl(c, h, 0), k) for (c, h), k in zip(inst, ks)]

    gs, ms = [None] * len(inst), [None] * len(inst)
    for h in range(H_C):
        m0 = m_all[:, h:h + 1]
        for c in range(nc):
            i = c * H_C + h
            gs[i] = bcs[i] + m0
            ms[i] = jnp.maximum(gs[i], dmaxs[i])
            m0 = ms[i][L - 1:L, :]
        m_all = jnp.where(lane == h, m0, m_all)
    m_last = [m[L - 1:L, :] for m in ms]
    decays = [jnp.exp(g[L - 1:L, :] - ml) for g, ml in zip(gs, m_last)]
    kds = [k * jnp.exp(bc[L - 1:L, :] - bc + ig - ml) for k, bc, ig, ml in zip(ks, bcs, igs, m_last)]
    kvs = [_dot(jnp.transpose(kd), sl(c, h, 2)) for (c, h), kd in zip(inst, kds)]
    ksums = [jnp.sum(kd, axis=0, keepdims=True) for kd in kds]

    cms, nrows = [None] * len(inst), [None] * len(inst)
    for h in range(H_C):
        cm, nrow = c_scr[h], n_scr[h:h + 1, :]
        for c in range(nc):
            i = c * H_C + h
            cms[i], nrows[i] = cm, nrow
            cm = decays[i] * cm + kvs[i]
            nrow = decays[i] * nrow + ksums[i]
        c_scr[h] = cm
        n_scr[h:h + 1, :] = nrow

    inters = [jnp.exp(g - m) for g, m in zip(gs, ms)]
    qks = [r * jnp.exp(d - m) for r, d, m in zip(qk_raw, dmats, ms)]
    nums = [it * _dot(sl(c, h, 0), cm) + _dot(qk, sl(c, h, 2))
            for (c, h), it, cm, qk in zip(inst, inters, cms, qks)]
    qns = [jnp.sum(sl(c, h, 0) * nrow, axis=-1, keepdims=True) for (c, h), nrow in zip(inst, nrows)]
    qksums = [jnp.sum(qk, axis=-1, keepdims=True) for qk in qks]
    dens = [it * qn + s for it, qn, s in zip(inters, qns, qksums)]
    hhs = [num / jnp.maximum(jnp.abs(den), jnp.exp(-m)) for num, den, m in zip(nums, dens, ms)]
    mss = [jnp.mean(hh * hh, axis=-1, keepdims=True) for hh in hhs]
    for (c, h), hh, msq in zip(inst, hhs, mss):
        o_ref[0, c * L:(c + 1) * L, h * DH_C:(h + 1) * DH_C] = (
            hh * lax.rsqrt(msq + EPS) * ng * _sigmoid(sl(c, h, 3)))
    m_scr[...] = m_all
    co_ref[0] = c_scr[...]
    no_ref[0] = n_scr[...]
    mo_ref[0] = m_all


def _mlstm(x, gates, c0, n0, m0, bi_row, bf_row, ng_row):
    b, t, _ = x.shape
    nc = min(MLSTM_CHUNKS, t // CHUNK)
    rows = nc * CHUNK
    blk = lambda bb, c: (bb, c, 0)
    per_b3 = lambda bb, c: (bb, 0, 0)
    per_b4 = lambda bb, c: (bb, 0, 0, 0)
    fixed = lambda bb, c: (0, 0)
    return pl.pallas_call(
        functools.partial(_mlstm_kernel, nc=nc),
        grid=(b, t // rows),
        in_specs=[pl.BlockSpec((1, rows, 4 * D_C), blk),
                  pl.BlockSpec((1, rows, LANES), blk),
                  pl.BlockSpec((1, H_C, DH_C, DH_C), per_b4),
                  pl.BlockSpec((1, H_C, DH_C), per_b3),
                  pl.BlockSpec((1, 1, LANES), per_b3),
                  pl.BlockSpec((1, LANES), fixed),
                  pl.BlockSpec((1, LANES), fixed),
                  pl.BlockSpec((1, DH_C), fixed)],
        out_specs=[pl.BlockSpec((1, rows, D_C), blk),
                   pl.BlockSpec((1, H_C, DH_C, DH_C), per_b4),
                   pl.BlockSpec((1, H_C, DH_C), per_b3),
                   pl.BlockSpec((1, 1, LANES), per_b3)],
        out_shape=[jax.ShapeDtypeStruct((b, t, D_C), F32),
                   jax.ShapeDtypeStruct((b, H_C, DH_C, DH_C), F32),
                   jax.ShapeDtypeStruct((b, H_C, DH_C), F32),
                   jax.ShapeDtypeStruct((b, 1, LANES), F32)],
        scratch_shapes=[pltpu.VMEM((H_C, DH_C, DH_C), F32),
                        pltpu.VMEM((H_C, DH_C), F32),
                        pltpu.VMEM((1, LANES), F32)],
        compiler_params=_params(("parallel", "arbitrary")),
        name="mlstm",
    )(x, gates, c0, n0, m0, bi_row, bf_row, ng_row)


def _outproj_kernel(x_ref, a_ref, b_ref, c_ref, w_ref, g_ref, o_ref):
    oa = jnp.concatenate([a_ref[hh] for hh in range(H_A)], axis=-1)
    mix = (_dot(oa, w_ref[0:D_A, :])
           + _dot(b_ref[...], w_ref[D_A:D_A + D_B, :])
           + _dot(c_ref[...], w_ref[D_A + D_B:, :]))
    o_ref[...] = x_ref[...] + _rms(mix, g_ref[...])


def _outproj(x, oa, ob, oc, w_out, g):
    n = x.shape[0]
    r = min(ROW_TILE, n)
    row = lambda i: (i, 0)
    fixed = lambda i: (0, 0)
    return pl.pallas_call(
        _outproj_kernel,
        grid=(n // r,),
        in_specs=[pl.BlockSpec((r, D_MODEL), row),
                  pl.BlockSpec((H_A, r, DH_A), lambda i: (0, i, 0)),
                  pl.BlockSpec((r, D_B), row),
                  pl.BlockSpec((r, D_C), row),
                  pl.BlockSpec(w_out.shape, fixed),
                  pl.BlockSpec((1, D_MODEL), fixed)],
        out_specs=pl.BlockSpec((r, D_MODEL), row),
        out_shape=jax.ShapeDtypeStruct((n, D_MODEL), F32),
        compiler_params=_params(("parallel",)),
        name="outproj",
    )(x, oa, ob, oc, w_out, g)


def _ffn_kernel(x_ref, hist_ref, g1_ref, wu_ref, cw_ref, wd_ref, g2_ref,
                o_ref, histo_ref, gp_scr, hist_scr, acc_scr, *, rows):
    t = pl.program_id(1)
    h0 = SUBLANES - (FFN_CONV - 1)

    @pl.when(t == 0)
    def _():
        hist_scr[h0:SUBLANES, :] = hist_ref[0]

    x = x_ref[0]
    h = _rms(x, g1_ref[...]).astype(BF16)
    acc_scr[...] = jnp.zeros_like(acc_scr)
    for ci in range(D_FF // FF_CHUNK):
        lo, hi = ci * FF_CHUNK, (ci + 1) * FF_CHUNK
        gate = jnp.dot(h, wu_ref[:, lo:hi], preferred_element_type=F32)
        up = jnp.dot(h, wu_ref[:, D_FF + lo:D_FF + hi], preferred_element_type=F32)
        gp_scr[h0:SUBLANES, :] = hist_scr[h0:SUBLANES, lo:hi]
        gp_scr[SUBLANES:SUBLANES + rows, :] = gate
        cw = cw_ref[:, lo:hi]
        conv = gate * cw[FFN_CONV - 1:FFN_CONV, :]
        for j in range(FFN_CONV - 1):
            conv = conv + gp_scr[h0 + j:h0 + j + rows, :] * cw[j:j + 1, :]
        hist_scr[h0:SUBLANES, lo:hi] = gp_scr[rows + h0:rows + SUBLANES, :]
        act = jax.nn.gelu(conv, approximate=True) * up
        acc_scr[...] += jnp.dot(act.astype(BF16), wd_ref[lo:hi, :], preferred_element_type=F32)
    histo_ref[0] = hist_scr[h0:SUBLANES, :]
    o_ref[0] = x + _rms(acc_scr[...], g2_ref[...])


def _ffn(x, hist, g1, w_up, conv_w, w_down, g2):
    b, t, _ = x.shape
    r = min(ROW_TILE, t)
    blk = lambda bb, i: (bb, i, 0)
    per_b = lambda bb, i: (bb, 0, 0)
    fixed = lambda bb, i: (0, 0)
    kern = functools.partial(_ffn_kernel, rows=r)
    return pl.pallas_call(
        kern,
        grid=(b, t // r),
        in_specs=[pl.BlockSpec((1, r, D_MODEL), blk),
                  pl.BlockSpec((1, FFN_CONV - 1, D_FF), per_b),
                  pl.BlockSpec((1, D_MODEL), fixed),
                  pl.BlockSpec(w_up.shape, fixed),
                  pl.BlockSpec((FFN_CONV, D_FF), fixed),
                  pl.BlockSpec(w_down.shape, fixed),
                  pl.BlockSpec((1, D_MODEL), fixed)],
        out_specs=[pl.BlockSpec((1, r, D_MODEL), blk),
                   pl.BlockSpec((1, FFN_CONV - 1, D_FF), per_b)],
        out_shape=[jax.ShapeDtypeStruct((b, t, D_MODEL), F32),
                   jax.ShapeDtypeStruct((b, FFN_CONV - 1, D_FF), F32)],
        scratch_shapes=[pltpu.VMEM((SUBLANES + r, FF_CHUNK), F32),
                        pltpu.VMEM((SUBLANES, D_FF), F32),
                        pltpu.VMEM((r, D_MODEL), F32)],
        compiler_params=_params(("parallel", "arbitrary")),
        name="ffn",
    )(x, hist, g1, w_up, conv_w, w_down, g2)


def _lane_row(vals, offset):
    return jnp.zeros((1, LANES), F32).at[0, offset:offset + vals.shape[0]].set(vals.astype(F32))


def _prep_weights(w_in, w_out, ffn_w_up, ffn_w_down):
    c_gate0 = 3 * D_A + 4 * D_B
    c_c0 = c_gate0 + 2 * H_B
    c_gate1 = c_c0 + 4 * D_C
    wb = w_in.astype(BF16)
    w_main = jnp.concatenate([wb[:, :c_gate0], wb[:, c_c0:c_gate1]], axis=1)
    n_gate = 2 * H_B + 2 * H_C
    w_gate = jnp.concatenate([wb[:, c_gate0:c_c0], wb[:, c_gate1:],
                              jnp.zeros((D_MODEL, LANES - n_gate), BF16)], axis=1)
    return w_main, w_gate, w_out.astype(BF16), ffn_w_up.astype(BF16), ffn_w_down.astype(BF16)


def _layer(x, kv_k, kv_v, gdn_hist, gdn_s, m_c, m_n, m_m, ffn_hist, lw):
    (g_mix_pre, g_mix_post, g_ffn_pre, g_ffn_post, w_main, w_gate, gdn_conv_w, alog_row, dtb_row,
     gdn_ng, bi_row, bf_row, mlstm_ng, w_out, w_up, ffn_conv_w, w_down) = lw
    b, t, _ = x.shape
    n = b * t
    xf = x.reshape(n, D_MODEL)
    q_hm, kt_hm, v_hm, k_f, v_f, qkv_b, z_b, qkvo_c, gates = _inproj(xf, g_mix_pre, w_main, w_gate)

    def streams_first(a):
        return a.reshape(H_A, b, t, DH_A).transpose(1, 0, 2, 3)

    k_new, v_new = streams_first(k_f), streams_first(v_f)
    g = b * H_A
    past = 0 if kv_k is None else kv_k.shape[2]
    tk_total = -(-(past + t) // ATTN_TK) * ATTN_TK
    pad = tk_total - (past + t)
    if kv_k is None:
        kt_all = kt_hm.reshape(H_A, DH_A, b, t).transpose(2, 0, 1, 3)
        v_all = streams_first(v_hm)
        if pad > 0:
            kt_all = jnp.pad(kt_all, ((0, 0), (0, 0), (0, 0), (0, pad)))
            v_all = jnp.pad(v_all, ((0, 0), (0, 0), (0, pad), (0, 0)))
    else:
        zpad = [jnp.zeros((b, H_A, pad, DH_A), BF16)] if pad > 0 else []
        k_all = jnp.concatenate([kv_k.astype(BF16), k_new.astype(BF16)] + zpad, axis=2)
        v_all = jnp.concatenate([kv_v.astype(BF16), streams_first(v_hm)] + zpad, axis=2)
        kt_all = k_all.transpose(0, 1, 3, 2)
    o_a = _attention(streams_first(q_hm).reshape(g, t, DH_A), kt_all.reshape(g, DH_A, tk_total),
                     v_all.reshape(g, tk_total, DH_A), past)
    o_a = o_a.reshape(b, H_A, t, DH_A).transpose(1, 0, 2, 3).reshape(H_A, n, DH_A)

    o_b, gdn_hist_new, s_new = _gdn(qkv_b.reshape(b, t, 3 * D_B), z_b.reshape(b, t, D_B),
                                    gates.reshape(b, t, LANES), gdn_hist, gdn_s, gdn_conv_w,
                                    alog_row, dtb_row, gdn_ng)

    o_c, c_new, n_new, m_new = _mlstm(qkvo_c.reshape(b, t, 4 * D_C), gates.reshape(b, t, LANES),
                                      m_c, m_n, m_m, bi_row, bf_row, mlstm_ng)

    x1 = _outproj(xf, o_a, o_b.reshape(n, D_B), o_c.reshape(n, D_C), w_out, g_mix_post)
    x2, ffn_hist_new = _ffn(x1.reshape(b, t, D_MODEL), ffn_hist, g_ffn_pre, w_up, ffn_conv_w,
                            w_down, g_ffn_post)
    return (x2, k_new, v_new, gdn_hist_new, s_new, c_new, n_new, m_new[:, 0, :H_C], ffn_hist_new)


def kernel(x_prompt, x_sample, cache_sb_k, cache_sb_v, state_gdn_conv, state_gdn_s, state_mlstm_c, state_mlstm_n, state_mlstm_m, state_ffn_conv, g_mix_pre, g_mix_post, g_ffn_pre, g_ffn_post, w_in, gdn_conv_w, gdn_a_log, gdn_dt_bias, gdn_norm_g, mlstm_b_i, mlstm_b_f, mlstm_norm_g, w_out, ffn_w_up, ffn_conv_w, ffn_w_down):
    depth = w_in.shape[0]
    bp = x_prompt.shape[0]
    bs = x_sample.shape[0]
    zero_gdn_hist = jnp.zeros((bp, GDN_CONV - 1, 3 * D_B), F32)
    zero_s = jnp.zeros((bp, H_B, DH_B, DH_B), F32)
    zero_c = jnp.zeros((bp, H_C, DH_C, DH_C), F32)
    zero_n = jnp.zeros((bp, H_C, DH_C), F32)
    m_init = jnp.full((bp, 1, LANES), NEG, F32)
    zero_ffn_hist = jnp.zeros((bp, FFN_CONV - 1, D_FF), F32)

    xp, xs = x_prompt, x_sample
    new_p, new_s = [], []
    for l in range(depth):
        w_main, w_gate, w_o, w_u, w_d = _prep_weights(w_in[l], w_out[l], ffn_w_up[l], ffn_w_down[l])
        lw = (g_mix_pre[l][None], g_mix_post[l][None], g_ffn_pre[l][None], g_ffn_post[l][None],
              w_main, w_gate, gdn_conv_w[l], _lane_row(gdn_a_log[l], G_A),
              _lane_row(gdn_dt_bias[l], G_A), gdn_norm_g[l][None],
              _lane_row(mlstm_b_i[l], G_I), _lane_row(mlstm_b_f[l], G_F), mlstm_norm_g[l][None],
              w_o, w_u, ffn_conv_w[l], w_d)
        xp, *st_p = _layer(xp, None, None, zero_gdn_hist, zero_s, zero_c, zero_n, m_init,
                           zero_ffn_hist, lw)
        m_s = jnp.zeros((bs, 1, LANES), F32).at[:, 0, :H_C].set(state_mlstm_m[l])
        xs, *st_s = _layer(xs, cache_sb_k[l], cache_sb_v[l], state_gdn_conv[l], state_gdn_s[l],
                           state_mlstm_c[l], state_mlstm_n[l], m_s, state_ffn_conv[l], lw)
        new_p.append(st_p)
        new_s.append(st_s)
    outs_p = [jnp.stack(a) for a in zip(*new_p)]
    outs_s = [jnp.stack(a) for a in zip(*new_s)]
    return (xp, xs, *outs_p, *outs_s)
```

```python
import functools

import jax
import jax.numpy as jnp
import numpy as np
from jax import lax
from jax.experimental import pallas as pl
from jax.experimental.pallas import tpu as pltpu

F32 = jnp.float32
BF16 = jnp.bfloat16

D_MODEL = 1024
CHUNK = 64
H_A, DH_A = 4, 64
H_B, DH_B = 4, 64
H_C, DH_C = 4, 128
D_A, D_B, D_C = H_A * DH_A, H_B * DH_B, H_C * DH_C
GDN_CONV = 4
D_FF = 2816
FFN_CONV = 3
EPS = 1e-6
NEG = -1e30

LANES = 128
SUBLANES = 8
ROW_TILE = 512
ATTN_TQ = 1024
ATTN_TK = 256
ATTN_UNROLL = 4
FF_CHUNK = 2816
GDN_CHUNKS = 4
MLSTM_CHUNKS = 4
VMEM_LIMIT = 56 * 1024 * 1024

G_BETA, G_A, G_I, G_F = 0, 4, 8, 12


def _params(sem):
    return pltpu.CompilerParams(dimension_semantics=sem, vmem_limit_bytes=VMEM_LIMIT)


def _dot(a, b):
    return jnp.dot(a.astype(BF16), b.astype(BF16), preferred_element_type=F32)


def _dot_nt(a, b):
    return lax.dot_general(a.astype(BF16), b.astype(BF16), (((1,), (1,)), ((), ())),
                           preferred_element_type=F32)


def _split(a):
    hi = a.astype(BF16)
    lo = (a - hi.astype(F32)).astype(BF16)
    return hi, lo


def _dot3(a, b):
    ah, al = _split(a)
    bh, bl = _split(b)
    return (jnp.dot(ah, bh, preferred_element_type=F32)
            + jnp.dot(ah, bl, preferred_element_type=F32)
            + jnp.dot(al, bh, preferred_element_type=F32))


def _dot_exact_lhs(a01, b):
    a = a01.astype(BF16)
    b0 = b.astype(BF16)
    r1 = b - b0.astype(F32)
    b1 = r1.astype(BF16)
    b2 = (r1 - b1.astype(F32)).astype(BF16)
    return (jnp.dot(a, b0, preferred_element_type=F32)
            + jnp.dot(a, b1, preferred_element_type=F32)
            + jnp.dot(a, b2, preferred_element_type=F32))


def _sigmoid(x):
    return 1.0 / (1.0 + jnp.exp(-x))


def _softplus(x):
    return jnp.maximum(x, 0.0) + jnp.log(1.0 + jnp.exp(-jnp.abs(x)))


def _rms(x, g):
    return x * lax.rsqrt(jnp.mean(x * x, axis=-1, keepdims=True) + EPS) * g


def _inproj_kernel(x_ref, g_ref, w_ref, wg_ref, q_ref, kt_ref, v_ref, kf_ref, vf_ref,
                   b_ref, z_ref, c_ref, gt_ref):
    h = _rms(x_ref[...], g_ref[...]).astype(BF16)
    qkv = jnp.dot(h, w_ref[:, 0:3 * D_A], preferred_element_type=F32)
    for hh in range(H_A):
        lo = hh * DH_A
        q_ref[hh] = qkv[:, lo:lo + DH_A].astype(BF16)
        kf_ref[hh] = qkv[:, D_A + lo:D_A + lo + DH_A]
        v_h = qkv[:, 2 * D_A + lo:2 * D_A + lo + DH_A]
        vf_ref[hh] = v_h
        v_ref[hh] = v_h.astype(BF16)
    for pair in range(H_A * DH_A // LANES):
        kt_pair = jnp.transpose(qkv[:, D_A + pair * LANES:D_A + (pair + 1) * LANES])
        for sub in range(LANES // DH_A):
            kt_ref[pair * (LANES // DH_A) + sub] = kt_pair[sub * DH_A:(sub + 1) * DH_A, :].astype(BF16)
    b_ref[...] = jnp.dot(h, w_ref[:, 768:1536], preferred_element_type=F32)
    z_ref[...] = jnp.dot(h, w_ref[:, 1536:1792], preferred_element_type=F32)
    c_ref[...] = jnp.dot(h, w_ref[:, 1792:3840], preferred_element_type=F32)
    gt_ref[...] = jnp.dot(h, wg_ref[...], preferred_element_type=F32)


def _inproj(x, g, w_main, w_gate):
    n = x.shape[0]
    r = min(ROW_TILE, n)
    assert n % r == 0
    widths = (3 * D_B, D_B, 4 * D_C, LANES)
    row = lambda i: (i, 0)
    fixed = lambda i: (0, 0)
    head_rows = lambda i: (0, i, 0)
    head_cols = lambda i: (0, 0, i)
    hm_spec = pl.BlockSpec((H_A, r, DH_A), head_rows)
    hm = lambda dt: jax.ShapeDtypeStruct((H_A, n, DH_A), dt)
    return pl.pallas_call(
        _inproj_kernel,
        grid=(n // r,),
        in_specs=[pl.BlockSpec((r, D_MODEL), row),
                  pl.BlockSpec((1, D_MODEL), fixed),
                  pl.BlockSpec(w_main.shape, fixed),
                  pl.BlockSpec(w_gate.shape, fixed)],
        out_specs=[hm_spec, pl.BlockSpec((H_A, DH_A, r), head_cols), hm_spec, hm_spec, hm_spec]
                  + [pl.BlockSpec((r, w), row) for w in widths],
        out_shape=[hm(BF16), jax.ShapeDtypeStruct((H_A, DH_A, n), BF16), hm(BF16), hm(F32), hm(F32)]
                  + [jax.ShapeDtypeStruct((n, w), F32) for w in widths],
        compiler_params=_params(("parallel",)),
        name="inproj",
    )(x, g, w_main, w_gate)


def _attn_body(q_ref, u_ref, o_ref, qs_scr, z_scr, sp_scr, a_scr, acc_scr, car_scr,
               qk_block, v_block, n_blk, *, tq, tk, unroll, q_start):
    i = pl.program_id(1)
    q_pos0 = q_start + i * tq
    n_all = (q_pos0 + tq - 1 + tk - 1) // tk
    n_full = q_pos0 // tk
    qs_scr[...] = q_ref[0] * jnp.asarray(DH_A ** -0.5, BF16)
    acc_scr[...] = jnp.zeros_like(acc_scr)
    car_scr[...] = jnp.zeros_like(car_scr)
    z_scr[1] = jnp.full((tq, tk), NEG, BF16)
    sp_scr[1] = jnp.zeros((tq, tk), BF16)
    a_scr[0] = jnp.zeros((tq, tk), BF16)

    def stage_qk(j, masked, slot):
        ks = pl.multiple_of(jnp.clip(j, 0, n_blk - 1) * tk, tk)
        z = qk_block(qs_scr[...], ks)
        if masked:
            kpos = j * tk + lax.broadcasted_iota(jnp.int32, (tq, tk), 1)
            qpos = q_pos0 + lax.broadcasted_iota(jnp.int32, (tq, tk), 0)
            z = jnp.where(kpos < qpos, z, NEG)
        zb = z.astype(BF16)
        z_scr[slot] = zb
        sp_scr[slot] = _softplus(zb)

    def stage_exp(slot):
        mm = jnp.dot(sp_scr[slot], u_ref[...], preferred_element_type=F32)
        car = car_scr[...]
        rest = mm + jnp.concatenate([car] * (tk // LANES), axis=1)
        a_scr[slot] = jnp.exp(z_scr[slot] - rest.astype(BF16))
        car_scr[...] = car + jnp.broadcast_to(mm[:, 0:1], (tq, LANES))

    def stage_pv(j, slot):
        ks = pl.multiple_of(jnp.clip(j, 0, n_blk - 1) * tk, tk)
        acc_scr[...] += jnp.dot(a_scr[slot], v_block(ks), preferred_element_type=F32)

    def step(j, masked, slot):
        stage_qk(j, masked, slot)
        stage_exp(1 - slot)
        stage_pv(j + 2, slot)

    n_full_u = (n_full // unroll) * unroll
    n_diag_u = -(-(n_all - n_full_u) // unroll) * unroll

    def diag_body(t, c):
        j = n_full_u + n_diag_u - 1 - unroll * t
        for u in range(unroll):
            step(j - u, True, u % 2)
        return c

    def full_body(t, c):
        j = n_full_u - 1 - unroll * t
        for u in range(unroll):
            step(j - u, False, u % 2)
        return c

    lax.fori_loop(0, n_diag_u // unroll, diag_body, 0)
    lax.fori_loop(0, n_full_u // unroll, full_body, 0)
    stage_exp(1)
    stage_pv(1, 0)
    stage_pv(0, 1)
    o_ref[0] = acc_scr[...]


def _attn_kernel(q_ref, kt_ref, v_ref, u_ref, o_ref, *scratch, tk, **static):
    def qk_block(q, ks):
        return jnp.dot(q, kt_ref[0, :, pl.ds(ks, tk)], preferred_element_type=F32)

    def v_block(ks):
        return v_ref[0, pl.ds(ks, tk), :]

    _attn_body(q_ref, u_ref, o_ref, *scratch, qk_block, v_block, v_ref.shape[1] // tk, tk=tk, **static)


def _attn_cached_kernel(q_ref, kp_ref, vp_ref, kn_ref, vn_ref, u_ref, o_ref, *scratch, tk, **static):
    *pipe_scratch, k_scr, v_scr = scratch
    past, t = kp_ref.shape[1], kn_ref.shape[1]
    for dst, cached, new in ((k_scr, kp_ref, kn_ref), (v_scr, vp_ref, vn_ref)):
        dst[0:past, :] = cached[0].astype(BF16)
        dst[past:past + t, :] = new[0]
        dst[past + t:, :] = jnp.zeros((dst.shape[0] - past - t, dst.shape[1]), BF16)

    def qk_block(q, ks):
        return lax.dot_general(q, k_scr[pl.ds(ks, tk), :], (((1,), (1,)), ((), ())),
                               preferred_element_type=F32)

    def v_block(ks):
        return v_scr[pl.ds(ks, tk), :]

    _attn_body(q_ref, u_ref, o_ref, *pipe_scratch, qk_block, v_block, k_scr.shape[0] // tk, tk=tk,
               **static)


def _attn_call(kern, g, t, tq, tk, d, in_arrays, in_specs, extra_scratch):
    u = jnp.asarray((np.arange(tk)[:, None] >= np.arange(tk)[None, :]).astype(np.float32), BF16)
    return pl.pallas_call(
        kern,
        grid=(g, t // tq),
        in_specs=[pl.BlockSpec((1, tq, d), lambda h, i: (h, i, 0))] + in_specs
                 + [pl.BlockSpec(u.shape, lambda h, i: (0, 0))],
        out_specs=pl.BlockSpec((1, tq, d), lambda h, i: (h, i, 0)),
        out_shape=jax.ShapeDtypeStruct((g, t, d), F32),
        scratch_shapes=[pltpu.VMEM((tq, d), BF16), pltpu.VMEM((2, tq, tk), BF16),
                        pltpu.VMEM((2, tq, tk), BF16), pltpu.VMEM((2, tq, tk), BF16),
                        pltpu.VMEM((tq, d), F32), pltpu.VMEM((tq, LANES), F32)] + extra_scratch,
        compiler_params=_params(("parallel", "arbitrary")),
        name="attn",
    )(*in_arrays, u)


def _attn_static(t, q_start):
    tq, tk = min(ATTN_TQ, t), ATTN_TK
    assert t % tq == 0
    unroll = ATTN_UNROLL if tq >= ATTN_UNROLL * tk else 2
    return tq, tk, dict(tq=tq, tk=tk, unroll=unroll, q_start=q_start)


def _attention(q, kt, v):
    g, t, d = q.shape
    tk_total = v.shape[1]
    tq, tk, static = _attn_static(t, 0)
    whole = lambda h, i: (h, 0, 0)
    return _attn_call(functools.partial(_attn_kernel, **static), g, t, tq, tk, d, (q, kt, v),
                      [pl.BlockSpec((1, d, tk_total), whole), pl.BlockSpec((1, tk_total, d), whole)], [])


def _attention_cached(q, k_past, v_past, k_new, v_new):
    g, t, d = q.shape
    past = k_past.shape[1]
    tq, tk, static = _attn_static(t, past)
    tk_total = -(-(past + t) // tk) * tk
    whole = lambda h, i: (h, 0, 0)
    return _attn_call(functools.partial(_attn_cached_kernel, **static), g, t, tq, tk, d,
                      (q, k_past, v_past, k_new, v_new),
                      [pl.BlockSpec((1, past, d), whole), pl.BlockSpec((1, past, d), whole),
                       pl.BlockSpec((1, t, d), whole), pl.BlockSpec((1, t, d), whole)],
                      [pltpu.VMEM((tk_total, d), BF16), pltpu.VMEM((tk_total, d), BF16)])


def _gdn_kernel(x_ref, z_ref, gt_ref, hist_ref, s0_ref, cw_ref, alog_ref, dtb_ref, ng_ref,
                o_ref, histo_ref, so_ref, xp_scr, s_scr, *, nc):
    L = CHUNK
    rows = nc * L
    step = pl.program_id(1)
    h0 = SUBLANES - (GDN_CONV - 1)

    @pl.when(step == 0)
    def _():
        xp_scr[h0:SUBLANES, :] = hist_ref[0]
        s_scr[...] = s0_ref[0]

    x = x_ref[0]
    xp_scr[SUBLANES:SUBLANES + rows, :] = x
    cw = cw_ref[...]
    conv = x * cw[GDN_CONV - 1:GDN_CONV, :]
    xp = xp_scr[...]
    for j in range(GDN_CONV - 1):
        shifted = pltpu.roll(xp, SUBLANES + rows - (h0 + j), axis=0)[:rows, :]
        conv = conv + shifted * cw[j:j + 1, :]
    new_hist = xp_scr[rows + h0:rows + SUBLANES, :]
    xp_scr[h0:SUBLANES, :] = new_hist
    histo_ref[0] = new_hist
    act = conv * _sigmoid(conv)

    gates = gt_ref[0]
    beta_t = _sigmoid(gates)
    g_t = -jnp.exp(alog_ref[...]) * _softplus(gates + dtb_ref[...])
    rr = lax.broadcasted_iota(jnp.int32, (rows, rows), 0)
    cc = lax.broadcasted_iota(jnp.int32, (rows, rows), 1)
    same_chunk = (rr // L) == (cc // L)
    gc_t = _dot_exact_lhs((cc <= rr) & same_chunk, g_t)
    egc_t = jnp.exp(gc_t)
    gc_tt = jnp.transpose(gc_t)
    row = lax.broadcasted_iota(jnp.int32, (L, L), 0)
    col = lax.broadcasted_iota(jnp.int32, (L, L), 1)
    tri = col <= row
    strict = col < row
    eye = (col == row).astype(F32)
    z_all = z_ref[0]
    ng = ng_ref[...]

    inst = [(c, h) for c in range(nc) for h in range(H_B)]

    def part(c, h, p):
        return act[c * L:(c + 1) * L, p * D_B + h * DH_B:p * D_B + (h + 1) * DH_B]

    def col(tile, c, lane0, h):
        return tile[c * L:(c + 1) * L, lane0 + h:lane0 + h + 1]

    hr = lax.broadcasted_iota(jnp.int32, (D_B, D_B), 0) // DH_B
    hc = lax.broadcasted_iota(jnp.int32, (D_B, D_B), 1) // DH_B
    head_ones = (hr == hc).astype(BF16)

    def head_sumsq(x):
        hi, lo = _split(x * x)
        return (jnp.dot(hi, head_ones, preferred_element_type=F32)
                + jnp.dot(lo, head_ones, preferred_element_type=F32))

    q_all, k_all = act[:, 0:D_B], act[:, D_B:2 * D_B]
    qn_all = q_all * (lax.rsqrt(head_sumsq(q_all) + EPS) * (DH_B ** -0.5))
    kn_all = k_all * lax.rsqrt(head_sumsq(k_all) + EPS)
    qn = [qn_all[c * L:(c + 1) * L, h * DH_B:(h + 1) * DH_B] for c, h in inst]
    kn = [kn_all[c * L:(c + 1) * L, h * DH_B:(h + 1) * DH_B] for c, h in inst]
    betas = [col(beta_t, c, G_BETA, h) for c, h in inst]
    gcs = [col(gc_t, c, G_A, h) for c, h in inst]
    egcs = [col(egc_t, c, G_A, h) for c, h in inst]
    gams = [jnp.exp(jnp.where(tri, gc - gc_tt[G_A + h:G_A + h + 1, c * L:(c + 1) * L], -jnp.inf))
            for (c, h), gc in zip(inst, gcs)]
    gls = [gc[L - 1:L, :] for gc in gcs]
    kks = [_dot_nt(k, k) for k in kn]
    nn = [-jnp.where(strict, beta * kk * gam, 0.0) for beta, kk, gam in zip(betas, kks, gams)]
    x_inv = [eye + n for n in nn]
    pw = nn
    for _ in range(5):
        pw = [_dot(p, p) for p in pw]
        x_inv = [xi + _dot(xi, p) for xi, p in zip(x_inv, pw)]
    us = [_dot(xi, beta * part(c, h, 2)) for (c, h), xi, beta in zip(inst, x_inv, betas)]
    ws = [_dot(xi, (beta * egc) * k) for xi, beta, egc, k in zip(x_inv, betas, egcs, kn)]
    qks = [jnp.where(tri, _dot_nt(q, k) * gam, 0.0) for q, k, gam in zip(qn, kn, gams)]
    qes = [q * egc for q, egc in zip(qn, egcs)]
    kdts = [jnp.transpose(k * jnp.exp(gl - gc)) for k, gl, gc in zip(kn, gls, gcs)]
    egls = [jnp.exp(gl) for gl in gls]
    kdw = [_dot(kdt, w) for kdt, w in zip(kdts, ws)]
    drive = [_dot(kdt, u) for kdt, u in zip(kdts, us)]

    states = [None] * len(inst)
    state = [s_scr[h] for h in range(H_B)]
    for c in range(nc):
        for h in range(H_B):
            states[c * H_B + h] = state[h]
        state = [state[h] * egls[c * H_B + h] - _dot(kdw[c * H_B + h], state[h]) + drive[c * H_B + h]
                 for h in range(H_B)]
    v_new = [u - _dot(w, s) for u, w, s in zip(us, ws, states)]
    outs = [_dot(qe, s) + _dot(qk, vn) for qe, s, qk, vn in zip(qes, states, qks, v_new)]
    o_ms = [jnp.mean(o * o, axis=-1, keepdims=True) for o in outs]
    o_n = [o * lax.rsqrt(ms + EPS) * ng for o, ms in zip(outs, o_ms)]
    for c in range(nc):
        zc = z_all[c * L:(c + 1) * L, :]
        o_ref[0, c * L:(c + 1) * L, :] = (
            jnp.concatenate(o_n[c * H_B:(c + 1) * H_B], axis=-1) * (zc * _sigmoid(zc)))
    for h in range(H_B):
        s_scr[h] = state[h]
    so_ref[0] = s_scr[...]


def _gdn(qkv, z, gates, hist, s0, conv_w, alog_row, dtb_row, ng_row):
    b, t, _ = qkv.shape
    nc = min(GDN_CHUNKS, t // CHUNK)
    rows = nc * CHUNK
    blk = lambda bb, c: (bb, c, 0)
    per_b3 = lambda bb, c: (bb, 0, 0)
    per_b4 = lambda bb, c: (bb, 0, 0, 0)
    fixed = lambda bb, c: (0, 0)
    return pl.pallas_call(
        functools.partial(_gdn_kernel, nc=nc),
        grid=(b, t // rows),
        in_specs=[pl.BlockSpec((1, rows, 3 * D_B), blk),
                  pl.BlockSpec((1, rows, D_B), blk),
                  pl.BlockSpec((1, rows, LANES), blk),
                  pl.BlockSpec((1, GDN_CONV - 1, 3 * D_B), per_b3),
                  pl.BlockSpec((1, H_B, DH_B, DH_B), per_b4),
                  pl.BlockSpec((GDN_CONV, 3 * D_B), fixed),
                  pl.BlockSpec((1, LANES), fixed),
                  pl.BlockSpec((1, LANES), fixed),
                  pl.BlockSpec((1, DH_B), fixed)],
        out_specs=[pl.BlockSpec((1, rows, D_B), blk),
                   pl.BlockSpec((1, GDN_CONV - 1, 3 * D_B), per_b3),
                   pl.BlockSpec((1, H_B, DH_B, DH_B), per_b4)],
        out_shape=[jax.ShapeDtypeStruct((b, t, D_B), F32),
                   jax.ShapeDtypeStruct((b, GDN_CONV - 1, 3 * D_B), F32),
                   jax.ShapeDtypeStruct((b, H_B, DH_B, DH_B), F32)],
        scratch_shapes=[pltpu.VMEM((SUBLANES + rows, 3 * D_B), F32),
                        pltpu.VMEM((H_B, DH_B, DH_B), F32)],
        compiler_params=_params(("parallel", "arbitrary")),
        name="gdn",
    )(qkv, z, gates, hist, s0, conv_w, alog_row, dtb_row, ng_row)


def _mlstm_kernel(x_ref, gt_ref, c0_ref, n0_ref, m0_ref, bi_ref, bf_ref, ng_ref,
                  o_ref, co_ref, no_ref, mo_ref, c_scr, n_scr, m_scr, *, nc):
    L = CHUNK
    rows = nc * L
    step = pl.program_id(1)

    @pl.when(step == 0)
    def _():
        c_scr[...] = c0_ref[0]
        n_scr[...] = n0_ref[0]
        m_scr[...] = m0_ref[0]

    gates = gt_ref[0]
    ig_t = gates + bi_ref[...]
    xf = gates + bf_ref[...]
    lf_t = jnp.minimum(xf, 0.0) - jnp.log(1.0 + jnp.exp(-jnp.abs(xf)))
    rr = lax.broadcasted_iota(jnp.int32, (rows, rows), 0)
    cc = lax.broadcasted_iota(jnp.int32, (rows, rows), 1)
    same_chunk = (rr // L) == (cc // L)
    bc_t = _dot_exact_lhs((cc <= rr) & same_chunk, lf_t)
    bc_tt = jnp.transpose(bc_t)
    ig_tt = jnp.transpose(ig_t)
    row = lax.broadcasted_iota(jnp.int32, (L, L), 0)
    col = lax.broadcasted_iota(jnp.int32, (L, L), 1)
    tri = col <= row
    lane = lax.broadcasted_iota(jnp.int32, (1, LANES), 1)
    m_all = m_scr[...]
    ng = ng_ref[...]

    inst = [(c, h) for c in range(nc) for h in range(H_C)]

    def sl(c, h, part):
        return x_ref[0, c * L:(c + 1) * L, part * D_C + h * DH_C:part * D_C + (h + 1) * DH_C]

    bcs = [bc_t[c * L:(c + 1) * L, G_F + h:G_F + h + 1] for c, h in inst]
    igs = [ig_t[c * L:(c + 1) * L, G_I + h:G_I + h + 1] for c, h in inst]
    dmats = [jnp.where(tri, bc - bc_tt[G_F + h:G_F + h + 1, c * L:(c + 1) * L]
                       + ig_tt[G_I + h:G_I + h + 1, c * L:(c + 1) * L], -jnp.inf)
             for (c, h), bc in zip(inst, bcs)]
    dmaxs = [jnp.max(d, axis=-1, keepdims=True) for d in dmats]
    ks = [sl(c, h, 1) * (DH_C ** -0.5) for c, h in inst]
    qk_raw = [_dot_nt(sl(c, h, 0), k) for (c, h), k in zip(inst, ks)]

    gs, ms = [None] * len(inst), [None] * len(inst)
    for h in range(H_C):
        m0 = m_all[:, h:h + 1]
        for c in range(nc):
            i = c * H_C + h
            gs[i] = bcs[i] + m0
            ms[i] = jnp.maximum(gs[i], dmaxs[i])
            m0 = ms[i][L - 1:L, :]
        m_all = jnp.where(lane == h, m0, m_all)
    m_last = [m[L - 1:L, :] for m in ms]
    decays = [jnp.exp(g[L - 1:L, :] - ml) for g, ml in zip(gs, m_last)]
    kds = [k * jnp.exp(bc[L - 1:L, :] - bc + ig - ml) for k, bc, ig, ml in zip(ks, bcs, igs, m_last)]
    kvs = [_dot(jnp.transpose(kd), sl(c, h, 2)) for (c, h), kd in zip(inst, kds)]
    ksums = [jnp.sum(kd, axis=0, keepdims=True) for kd in kds]

    cms, nrows = [None] * len(inst), [None] * len(inst)
    for h in range(H_C):
        cm, nrow = c_scr[h], n_scr[h:h + 1, :]
        for c in range(nc):
            i = c * H_C + h
            cms[i], nrows[i] = cm, nrow
            cm = decays[i] * cm + kvs[i]
            nrow = decays[i] * nrow + ksums[i]
        c_scr[h] = cm
        n_scr[h:h + 1, :] = nrow

    inters = [jnp.exp(g - m) for g, m in zip(gs, ms)]
    qks = [r * jnp.exp(d - m) for r, d, m in zip(qk_raw, dmats, ms)]
    nums = [it * _dot(sl(c, h, 0), cm) + _dot(qk, sl(c, h, 2))
            for (c, h), it, cm, qk in zip(inst, inters, cms, qks)]
    qns = [jnp.sum(sl(c, h, 0) * nrow, axis=-1, keepdims=True) for (c, h), nrow in zip(inst, nrows)]
    qksums = [jnp.sum(qk, axis=-1, keepdims=True) for qk in qks]
    dens = [it * qn + s for it, qn, s in zip(inters, qns, qksums)]
    hhs = [num / jnp.maximum(jnp.abs(den), jnp.exp(-m)) for num, den, m in zip(nums, dens, ms)]
    mss = [jnp.mean(hh * hh, axis=-1, keepdims=True) for hh in hhs]
    for (c, h), hh, msq in zip(inst, hhs, mss):
        o_ref[0, c * L:(c + 1) * L, h * DH_C:(h + 1) * DH_C] = (
            hh * lax.rsqrt(msq + EPS) * ng * _sigmoid(sl(c, h, 3)))
    m_scr[...] = m_all
    co_ref[0] = c_scr[...]
    no_ref[0] = n_scr[...]
    mo_ref[0] = m_all


def _mlstm(x, gates, c0, n0, m0, bi_row, bf_row, ng_row):
    b, t, _ = x.shape
    nc = min(MLSTM_CHUNKS, t // CHUNK)
    rows = nc * CHUNK
    blk = lambda bb, c: (bb, c, 0)
    per_b3 = lambda bb, c: (bb, 0, 0)
    per_b4 = lambda bb, c: (bb, 0, 0, 0)
    fixed = lambda bb, c: (0, 0)
    return pl.pallas_call(
        functools.partial(_mlstm_kernel, nc=nc),
        grid=(b, t // rows),
        in_specs=[pl.BlockSpec((1, rows, 4 * D_C), blk),
                  pl.BlockSpec((1, rows, LANES), blk),
                  pl.BlockSpec((1, H_C, DH_C, DH_C), per_b4),
                  pl.BlockSpec((1, H_C, DH_C), per_b3),
                  pl.BlockSpec((1, 1, LANES), per_b3),
                  pl.BlockSpec((1, LANES), fixed),
                  pl.BlockSpec((1, LANES), fixed),
                  pl.BlockSpec((1, DH_C), fixed)],
        out_specs=[pl.BlockSpec((1, rows, D_C), blk),
                   pl.BlockSpec((1, H_C, DH_C, DH_C), per_b4),
                   pl.BlockSpec((1, H_C, DH_C), per_b3),
                   pl.BlockSpec((1, 1, LANES), per_b3)],
        out_shape=[jax.ShapeDtypeStruct((b, t, D_C), F32),
                   jax.ShapeDtypeStruct((b, H_C, DH_C, DH_C), F32),
                   jax.ShapeDtypeStruct((b, H_C, DH_C), F32),
                   jax.ShapeDtypeStruct((b, 1, LANES), F32)],
        scratch_shapes=[pltpu.VMEM((H_C, DH_C, DH_C), F32),
                        pltpu.VMEM((H_C, DH_C), F32),
                        pltpu.VMEM((1, LANES), F32)],
        compiler_params=_params(("parallel", "arbitrary")),
        name="mlstm",
    )(x, gates, c0, n0, m0, bi_row, bf_row, ng_row)


def _outproj_kernel(x_ref, a_ref, b_ref, c_ref, w_ref, g_ref, o_ref):
    oa = jnp.concatenate([a_ref[hh] for hh in range(H_A)], axis=-1)
    mix = (_dot(oa, w_ref[0:D_A, :])
           + _dot(b_ref[...], w_ref[D_A:D_A + D_B, :])
           + _dot(c_ref[...], w_ref[D_A + D_B:, :]))
    o_ref[...] = x_ref[...] + _rms(mix, g_ref[...])


def _outproj(x, oa, ob, oc, w_out, g):
    n = x.shape[0]
    r = min(ROW_TILE, n)
    row = lambda i: (i, 0)
    fixed = lambda i: (0, 0)
    return pl.pallas_call(
        _outproj_kernel,
        grid=(n // r,),
        in_specs=[pl.BlockSpec((r, D_MODEL), row),
                  pl.BlockSpec((H_A, r, DH_A), lambda i: (0, i, 0)),
                  pl.BlockSpec((r, D_B), row),
                  pl.BlockSpec((r, D_C), row),
                  pl.BlockSpec(w_out.shape, fixed),
                  pl.BlockSpec((1, D_MODEL), fixed)],
        out_specs=pl.BlockSpec((r, D_MODEL), row),
        out_shape=jax.ShapeDtypeStruct((n, D_MODEL), F32),
        compiler_params=_params(("parallel",)),
        name="outproj",
    )(x, oa, ob, oc, w_out, g)


def _ffn_kernel(x_ref, hist_ref, g1_ref, wu_ref, cw_ref, wd_ref, g2_ref,
                o_ref, histo_ref, gp_scr, hist_scr, acc_scr, *, rows):
    t = pl.program_id(1)
    h0 = SUBLANES - (FFN_CONV - 1)

    @pl.when(t == 0)
    def _():
        hist_scr[h0:SUBLANES, :] = hist_ref[0]

    x = x_ref[0]
    h = _rms(x, g1_ref[...]).astype(BF16)
    acc_scr[...] = jnp.zeros_like(acc_scr)
    for ci in range(D_FF // FF_CHUNK):
        lo, hi = ci * FF_CHUNK, (ci + 1) * FF_CHUNK
        gate = jnp.dot(h, wu_ref[:, lo:hi], preferred_element_type=F32)
        up = jnp.dot(h, wu_ref[:, D_FF + lo:D_FF + hi], preferred_element_type=F32)
        gp_scr[h0:SUBLANES, :] = hist_scr[h0:SUBLANES, lo:hi]
        gp_scr[SUBLANES:SUBLANES + rows, :] = gate
        cw = cw_ref[:, lo:hi]
        conv = gate * cw[FFN_CONV - 1:FFN_CONV, :]
        for j in range(FFN_CONV - 1):
            conv = conv + gp_scr[h0 + j:h0 + j + rows, :] * cw[j:j + 1, :]
        hist_scr[h0:SUBLANES, lo:hi] = gp_scr[rows + h0:rows + SUBLANES, :]
        act = jax.nn.gelu(conv, approximate=True) * up
        acc_scr[...] += jnp.dot(act.astype(BF16), wd_ref[lo:hi, :], preferred_element_type=F32)
    histo_ref[0] = hist_scr[h0:SUBLANES, :]
    o_ref[0] = x + _rms(acc_scr[...], g2_ref[...])


def _ffn(x, hist, g1, w_up, conv_w, w_down, g2):
    b, t, _ = x.shape
    r = min(ROW_TILE, t)
    blk = lambda bb, i: (bb, i, 0)
    per_b = lambda bb, i: (bb, 0, 0)
    fixed = lambda bb, i: (0, 0)
    kern = functools.partial(_ffn_kernel, rows=r)
    return pl.pallas_call(
        kern,
        grid=(b, t // r),
        in_specs=[pl.BlockSpec((1, r, D_MODEL), blk),
                  pl.BlockSpec((1, FFN_CONV - 1, D_FF), per_b),
                  pl.BlockSpec((1, D_MODEL), fixed),
                  pl.BlockSpec(w_up.shape, fixed),
                  pl.BlockSpec((FFN_CONV, D_FF), fixed),
                  pl.BlockSpec(w_down.shape, fixed),
                  pl.BlockSpec((1, D_MODEL), fixed)],
        out_specs=[pl.BlockSpec((1, r, D_MODEL), blk),
                   pl.BlockSpec((1, FFN_CONV - 1, D_FF), per_b)],
        out_shape=[jax.ShapeDtypeStruct((b, t, D_MODEL), F32),
                   jax.ShapeDtypeStruct((b, FFN_CONV - 1, D_FF), F32)],
        scratch_shapes=[pltpu.VMEM((SUBLANES + r, FF_CHUNK), F32),
                        pltpu.VMEM((SUBLANES, D_FF), F32),
                        pltpu.VMEM((r, D_MODEL), F32)],
        compiler_params=_params(("parallel", "arbitrary")),
        name="ffn",
    )(x, hist, g1, w_up, conv_w, w_down, g2)


def _lane_row(vals, offset):
    return jnp.zeros((1, LANES), F32).at[0, offset:offset + vals.shape[0]].set(vals.astype(F32))


def _prep_weights(w_in, w_out, ffn_w_up, ffn_w_down):
    c_gate0 = 3 * D_A + 4 * D_B
    c_c0 = c_gate0 + 2 * H_B
    c_gate1 = c_c0 + 4 * D_C
    wb = w_in.astype(BF16)
    w_main = jnp.concatenate([wb[:, :c_gate0], wb[:, c_c0:c_gate1]], axis=1)
    n_gate = 2 * H_B + 2 * H_C
    w_gate = jnp.concatenate([wb[:, c_gate0:c_c0], wb[:, c_gate1:],
                              jnp.zeros((D_MODEL, LANES - n_gate), BF16)], axis=1)
    return w_main, w_gate, w_out.astype(BF16), ffn_w_up.astype(BF16), ffn_w_down.astype(BF16)


def _layer(x, kv_k, kv_v, gdn_hist, gdn_s, m_c, m_n, m_m, ffn_hist, lw):
    (g_mix_pre, g_mix_post, g_ffn_pre, g_ffn_post, w_main, w_gate, gdn_conv_w, alog_row, dtb_row,
     gdn_ng, bi_row, bf_row, mlstm_ng, w_out, w_up, ffn_conv_w, w_down) = lw
    b, t, _ = x.shape
    n = b * t
    xf = x.reshape(n, D_MODEL)
    q_hm, kt_hm, v_hm, k_f, v_f, qkv_b, z_b, qkvo_c, gates = _inproj(xf, g_mix_pre, w_main, w_gate)

    def streams_first(a):
        return a.reshape(H_A, b, t, DH_A).transpose(1, 0, 2, 3)

    k_new, v_new = streams_first(k_f), streams_first(v_f)
    g = b * H_A
    q_g = streams_first(q_hm).reshape(g, t, DH_A)
    if kv_k is None:
        pad = -(-t // ATTN_TK) * ATTN_TK - t
        kt_all = kt_hm.reshape(H_A, DH_A, b, t).transpose(2, 0, 1, 3)
        v_all = streams_first(v_hm)
        if pad > 0:
            kt_all = jnp.pad(kt_all, ((0, 0), (0, 0), (0, 0), (0, pad)))
            v_all = jnp.pad(v_all, ((0, 0), (0, 0), (0, pad), (0, 0)))
        o_a = _attention(q_g, kt_all.reshape(g, DH_A, t + pad), v_all.reshape(g, t + pad, DH_A))
    else:
        past = kv_k.shape[2]
        o_a = _attention_cached(q_g, kv_k.reshape(g, past, DH_A), kv_v.reshape(g, past, DH_A),
                                k_new.astype(BF16).reshape(g, t, DH_A),
                                streams_first(v_hm).reshape(g, t, DH_A))
    o_a = o_a.reshape(b, H_A, t, DH_A).transpose(1, 0, 2, 3).reshape(H_A, n, DH_A)

    o_b, gdn_hist_new, s_new = _gdn(qkv_b.reshape(b, t, 3 * D_B), z_b.reshape(b, t, D_B),
                                    gates.reshape(b, t, LANES), gdn_hist, gdn_s, gdn_conv_w,
                                    alog_row, dtb_row, gdn_ng)

    o_c, c_new, n_new, m_new = _mlstm(qkvo_c.reshape(b, t, 4 * D_C), gates.reshape(b, t, LANES),
                                      m_c, m_n, m_m, bi_row, bf_row, mlstm_ng)

    x1 = _outproj(xf, o_a, o_b.reshape(n, D_B), o_c.reshape(n, D_C), w_out, g_mix_post)
    x2, ffn_hist_new = _ffn(x1.reshape(b, t, D_MODEL), ffn_hist, g_ffn_pre, w_up, ffn_conv_w,
                            w_down, g_ffn_post)
    return (x2, k_new, v_new, gdn_hist_new, s_new, c_new, n_new, m_new[:, 0, :H_C], ffn_hist_new)


def kernel(x_prompt, x_sample, cache_sb_k, cache_sb_v, state_gdn_conv, state_gdn_s, state_mlstm_c, state_mlstm_n, state_mlstm_m, state_ffn_conv, g_mix_pre, g_mix_post, g_ffn_pre, g_ffn_post, w_in, gdn_conv_w, gdn_a_log, gdn_dt_bias, gdn_norm_g, mlstm_b_i, mlstm_b_f, mlstm_norm_g, w_out, ffn_w_up, ffn_conv_w, ffn_w_down):
    depth = w_in.shape[0]
    bp = x_prompt.shape[0]
    bs = x_sample.shape[0]
    zero_gdn_hist = jnp.zeros((bp, GDN_CONV - 1, 3 * D_B), F32)
    zero_s = jnp.zeros((bp, H_B, DH_B, DH_B), F32)
    zero_c = jnp.zeros((bp, H_C, DH_C, DH_C), F32)
    zero_n = jnp.zeros((bp, H_C, DH_C), F32)
    m_init = jnp.full((bp, 1, LANES), NEG, F32)
    zero_ffn_hist = jnp.zeros((bp, FFN_CONV - 1, D_FF), F32)

    xp, xs = x_prompt, x_sample
    new_p, new_s = [], []
    for l in range(depth):
        w_main, w_gate, w_o, w_u, w_d = _prep_weights(w_in[l], w_out[l], ffn_w_up[l], ffn_w_down[l])
        lw = (g_mix_pre[l][None], g_mix_post[l][None], g_ffn_pre[l][None], g_ffn_post[l][None],
              w_main, w_gate, gdn_conv_w[l], _lane_row(gdn_a_log[l], G_A),
              _lane_row(gdn_dt_bias[l], G_A), gdn_norm_g[l][None],
              _lane_row(mlstm_b_i[l], G_I), _lane_row(mlstm_b_f[l], G_F), mlstm_norm_g[l][None],
              w_o, w_u, ffn_conv_w[l], w_d)
        xp, *st_p = _layer(xp, None, None, zero_gdn_hist, zero_s, zero_c, zero_n, m_init,
                           zero_ffn_hist, lw)
        m_s = jnp.zeros((bs, 1, LANES), F32).at[:, 0, :H_C].set(state_mlstm_m[l])
        xs, *st_s = _layer(xs, cache_sb_k[l], cache_sb_v[l], state_gdn_conv[l], state_gdn_s[l],
                           state_mlstm_c[l], state_mlstm_n[l], m_s, state_ffn_conv[l], lw)
        new_p.append(st_p)
        new_s.append(st_s)
    outs_p = [jnp.stack(a) for a in zip(*new_p)]
    outs_s = [jnp.stack(a) for a in zip(*new_s)]
    return (xp, xs, *outs_p, *outs_s)
```

```python
import functools

import jax
import jax.numpy as jnp
import numpy as np
from jax import lax
from jax.experimental import pallas as pl
from jax.experimental.pallas import tpu as pltpu

F32 = jnp.float32
BF16 = jnp.bfloat16

D_MODEL = 1024
CHUNK = 64
H_A, DH_A = 4, 64
H_B, DH_B = 4, 64
H_C, DH_C = 4, 128
D_A, D_B, D_C = H_A * DH_A, H_B * DH_B, H_C * DH_C
GDN_CONV = 4
D_FF = 2816
FFN_CONV = 3
EPS = 1e-6
NEG = -1e30

LANES = 128
SUBLANES = 8
ROW_TILE = 512
ATTN_TQ = 1024
ATTN_TK = 256
ATTN_UNROLL = 4
FF_CHUNK = 2816
GDN_CHUNKS = 4
MLSTM_CHUNKS = 4
VMEM_LIMIT = 56 * 1024 * 1024

G_BETA, G_A, G_I, G_F = 0, 4, 8, 12


def _params(sem):
    return pltpu.CompilerParams(dimension_semantics=sem, vmem_limit_bytes=VMEM_LIMIT)


def _dot(a, b):
    return jnp.dot(a.astype(BF16), b.astype(BF16), preferred_element_type=F32)


def _dot_nt(a, b):
    return lax.dot_general(a.astype(BF16), b.astype(BF16), (((1,), (1,)), ((), ())),
                           preferred_element_type=F32)


def _split(a):
    hi = a.astype(BF16)
    lo = (a - hi.astype(F32)).astype(BF16)
    return hi, lo


def _dot3(a, b):
    ah, al = _split(a)
    bh, bl = _split(b)
    return (jnp.dot(ah, bh, preferred_element_type=F32)
            + jnp.dot(ah, bl, preferred_element_type=F32)
            + jnp.dot(al, bh, preferred_element_type=F32))


def _dot_exact_lhs(a01, b):
    a = a01.astype(BF16)
    b0 = b.astype(BF16)
    r1 = b - b0.astype(F32)
    b1 = r1.astype(BF16)
    b2 = (r1 - b1.astype(F32)).astype(BF16)
    return (jnp.dot(a, b0, preferred_element_type=F32)
            + jnp.dot(a, b1, preferred_element_type=F32)
            + jnp.dot(a, b2, preferred_element_type=F32))


def _sigmoid(x):
    return 1.0 / (1.0 + jnp.exp(-x))


def _softplus(x):
    return jnp.maximum(x, 0.0) + jnp.log(1.0 + jnp.exp(-jnp.abs(x)))


def _rms(x, g):
    return x * lax.rsqrt(jnp.mean(x * x, axis=-1, keepdims=True) + EPS) * g


def _inproj_kernel(x_ref, g_ref, w_ref, wg_ref, q_ref, kt_ref, v_ref, kf_ref, vf_ref,
                   b_ref, z_ref, c_ref, gt_ref):
    h = _rms(x_ref[...], g_ref[...]).astype(BF16)
    qkv = jnp.dot(h, w_ref[:, 0:3 * D_A], preferred_element_type=F32)
    for hh in range(H_A):
        lo = hh * DH_A
        q_ref[hh] = qkv[:, lo:lo + DH_A].astype(BF16)
        kf_ref[hh] = qkv[:, D_A + lo:D_A + lo + DH_A]
        v_h = qkv[:, 2 * D_A + lo:2 * D_A + lo + DH_A]
        vf_ref[hh] = v_h
        v_ref[hh] = v_h.astype(BF16)
    for pair in range(H_A * DH_A // LANES):
        kt_pair = jnp.transpose(qkv[:, D_A + pair * LANES:D_A + (pair + 1) * LANES])
        for sub in range(LANES // DH_A):
            kt_ref[pair * (LANES // DH_A) + sub] = kt_pair[sub * DH_A:(sub + 1) * DH_A, :].astype(BF16)
    b_ref[...] = jnp.dot(h, w_ref[:, 768:1536], preferred_element_type=F32)
    z_ref[...] = jnp.dot(h, w_ref[:, 1536:1792], preferred_element_type=F32)
    c_ref[...] = jnp.dot(h, w_ref[:, 1792:3840], preferred_element_type=F32)
    gt_ref[...] = jnp.dot(h, wg_ref[...], preferred_element_type=F32)


def _inproj(x, g, w_main, w_gate):
    n = x.shape[0]
    r = min(ROW_TILE, n)
    assert n % r == 0
    widths = (3 * D_B, D_B, 4 * D_C, LANES)
    row = lambda i: (i, 0)
    fixed = lambda i: (0, 0)
    head_rows = lambda i: (0, i, 0)
    head_cols = lambda i: (0, 0, i)
    hm_spec = pl.BlockSpec((H_A, r, DH_A), head_rows)
    hm = lambda dt: jax.ShapeDtypeStruct((H_A, n, DH_A), dt)
    return pl.pallas_call(
        _inproj_kernel,
        grid=(n // r,),
        in_specs=[pl.BlockSpec((r, D_MODEL), row),
                  pl.BlockSpec((1, D_MODEL), fixed),
                  pl.BlockSpec(w_main.shape, fixed),
                  pl.BlockSpec(w_gate.shape, fixed)],
        out_specs=[hm_spec, pl.BlockSpec((H_A, DH_A, r), head_cols), hm_spec, hm_spec, hm_spec]
                  + [pl.BlockSpec((r, w), row) for w in widths],
        out_shape=[hm(BF16), jax.ShapeDtypeStruct((H_A, DH_A, n), BF16), hm(BF16), hm(F32), hm(F32)]
                  + [jax.ShapeDtypeStruct((n, w), F32) for w in widths],
        compiler_params=_params(("parallel",)),
        name="inproj",
    )(x, g, w_main, w_gate)


def _attn_body(q_ref, u_ref, o_ref, qs_scr, z_scr, sp_scr, a_scr, acc_scr, car_scr,
               qk_block, v_block, n_blk, *, tq, tk, unroll, q_start):
    i = pl.program_id(1)
    q_pos0 = q_start + i * tq
    n_all = (q_pos0 + tq - 1 + tk - 1) // tk
    n_full = q_pos0 // tk
    qs_scr[...] = q_ref[0] * jnp.asarray(DH_A ** -0.5, BF16)
    acc_scr[...] = jnp.zeros_like(acc_scr)
    car_scr[...] = jnp.zeros_like(car_scr)
    z_scr[1] = jnp.full((tq, tk), NEG, BF16)
    sp_scr[1] = jnp.zeros((tq, tk), BF16)
    a_scr[0] = jnp.zeros((tq, tk), BF16)

    def stage_qk(j, masked, slot):
        ks = pl.multiple_of(jnp.clip(j, 0, n_blk - 1) * tk, tk)
        z = qk_block(qs_scr[...], ks)
        if masked:
            kpos = j * tk + lax.broadcasted_iota(jnp.int32, (tq, tk), 1)
            qpos = q_pos0 + lax.broadcasted_iota(jnp.int32, (tq, tk), 0)
            z = jnp.where(kpos < qpos, z, NEG)
        zb = z.astype(BF16)
        z_scr[slot] = zb
        sp_scr[slot] = _softplus(zb)

    def stage_exp(slot):
        mm = jnp.dot(sp_scr[slot], u_ref[...], preferred_element_type=F32)
        car = car_scr[...]
        rest = mm + jnp.concatenate([car] * (tk // LANES), axis=1)
        a_scr[slot] = jnp.exp(z_scr[slot] - rest.astype(BF16))
        car_scr[...] = car + jnp.broadcast_to(mm[:, 0:1], (tq, LANES))

    def stage_pv(j, slot):
        ks = pl.multiple_of(jnp.clip(j, 0, n_blk - 1) * tk, tk)
        acc_scr[...] += jnp.dot(a_scr[slot], v_block(ks), preferred_element_type=F32)

    def step(j, masked, slot):
        stage_qk(j, masked, slot)
        stage_exp(1 - slot)
        stage_pv(j + 2, slot)

    n_full_u = (n_full // unroll) * unroll
    n_diag_u = -(-(n_all - n_full_u) // unroll) * unroll

    def diag_body(t, c):
        j = n_full_u + n_diag_u - 1 - unroll * t
        for u in range(unroll):
            step(j - u, True, u % 2)
        return c

    def full_body(t, c):
        j = n_full_u - 1 - unroll * t
        for u in range(unroll):
            step(j - u, False, u % 2)
        return c

    lax.fori_loop(0, n_diag_u // unroll, diag_body, 0)
    lax.fori_loop(0, n_full_u // unroll, full_body, 0)
    stage_exp(1)
    stage_pv(1, 0)
    stage_pv(0, 1)
    o_ref[0] = acc_scr[...]


def _attn_kernel(q_ref, kt_ref, v_ref, u_ref, o_ref, *scratch, tk, **static):
    def qk_block(q, ks):
        return jnp.dot(q, kt_ref[0, :, pl.ds(ks, tk)], preferred_element_type=F32)

    def v_block(ks):
        return v_ref[0, pl.ds(ks, tk), :]

    _attn_body(q_ref, u_ref, o_ref, *scratch, qk_block, v_block, v_ref.shape[1] // tk, tk=tk, **static)


def _attn_cached_kernel(q_ref, kp_ref, vp_ref, kn_ref, vn_ref, u_ref, o_ref, *scratch, tk, **static):
    *pipe_scratch, k_scr, v_scr = scratch
    past, t = kp_ref.shape[1], kn_ref.shape[1]
    for dst, cached, new in ((k_scr, kp_ref, kn_ref), (v_scr, vp_ref, vn_ref)):
        dst[0:past, :] = cached[0].astype(BF16)
        dst[past:past + t, :] = new[0]
        dst[past + t:, :] = jnp.zeros((dst.shape[0] - past - t, dst.shape[1]), BF16)

    def qk_block(q, ks):
        return lax.dot_general(q, k_scr[pl.ds(ks, tk), :], (((1,), (1,)), ((), ())),
                               preferred_element_type=F32)

    def v_block(ks):
        return v_scr[pl.ds(ks, tk), :]

    _attn_body(q_ref, u_ref, o_ref, *pipe_scratch, qk_block, v_block, k_scr.shape[0] // tk, tk=tk,
               **static)


def _attn_call(kern, g, t, tq, tk, d, in_arrays, in_specs, extra_scratch):
    u = jnp.asarray((np.arange(tk)[:, None] >= np.arange(tk)[None, :]).astype(np.float32), BF16)
    return pl.pallas_call(
        kern,
        grid=(g, t // tq),
        in_specs=[pl.BlockSpec((1, tq, d), lambda h, i: (h, i, 0))] + in_specs
                 + [pl.BlockSpec(u.shape, lambda h, i: (0, 0))],
        out_specs=pl.BlockSpec((1, tq, d), lambda h, i: (h, i, 0)),
        out_shape=jax.ShapeDtypeStruct((g, t, d), F32),
        scratch_shapes=[pltpu.VMEM((tq, d), BF16), pltpu.VMEM((2, tq, tk), BF16),
                        pltpu.VMEM((2, tq, tk), BF16), pltpu.VMEM((2, tq, tk), BF16),
                        pltpu.VMEM((tq, d), F32), pltpu.VMEM((tq, LANES), F32)] + extra_scratch,
        compiler_params=_params(("parallel", "arbitrary")),
        name="attn",
    )(*in_arrays, u)


def _attn_static(t, q_start):
    tq, tk = min(ATTN_TQ, t), ATTN_TK
    assert t % tq == 0
    unroll = ATTN_UNROLL if tq >= ATTN_UNROLL * tk else 2
    return tq, tk, dict(tq=tq, tk=tk, unroll=unroll, q_start=q_start)


def _attention(q, kt, v):
    g, t, d = q.shape
    tk_total = v.shape[1]
    tq, tk, static = _attn_static(t, 0)
    whole = lambda h, i: (h, 0, 0)
    return _attn_call(functools.partial(_attn_kernel, **static), g, t, tq, tk, d, (q, kt, v),
                      [pl.BlockSpec((1, d, tk_total), whole), pl.BlockSpec((1, tk_total, d), whole)], [])


def _attention_cached(q, k_past, v_past, k_new, v_new):
    g, t, d = q.shape
    past = k_past.shape[1]
    tq, tk, static = _attn_static(t, past)
    tk_total = -(-(past + t) // tk) * tk
    whole = lambda h, i: (h, 0, 0)
    return _attn_call(functools.partial(_attn_cached_kernel, **static), g, t, tq, tk, d,
                      (q, k_past, v_past, k_new, v_new),
                      [pl.BlockSpec((1, past, d), whole), pl.BlockSpec((1, past, d), whole),
                       pl.BlockSpec((1, t, d), whole), pl.BlockSpec((1, t, d), whole)],
                      [pltpu.VMEM((tk_total, d), BF16), pltpu.VMEM((tk_total, d), BF16)])


def _gdn_kernel(x_ref, z_ref, gt_ref, hist_ref, s0_ref, cw_ref, alog_ref, dtb_ref, ng_ref,
                o_ref, histo_ref, so_ref, xp_scr, s_scr, *, nc):
    L = CHUNK
    rows = nc * L
    step = pl.program_id(1)
    h0 = SUBLANES - (GDN_CONV - 1)

    @pl.when(step == 0)
    def _():
        xp_scr[h0:SUBLANES, :] = hist_ref[0]
        s_scr[...] = s0_ref[0]

    x = x_ref[0]
    xp_scr[SUBLANES:SUBLANES + rows, :] = x
    cw = cw_ref[...]
    conv = x * cw[GDN_CONV - 1:GDN_CONV, :]
    xp = xp_scr[...]
    for j in range(GDN_CONV - 1):
        shifted = pltpu.roll(xp, SUBLANES + rows - (h0 + j), axis=0)[:rows, :]
        conv = conv + shifted * cw[j:j + 1, :]
    new_hist = xp_scr[rows + h0:rows + SUBLANES, :]
    xp_scr[h0:SUBLANES, :] = new_hist
    histo_ref[0] = new_hist
    act = conv * _sigmoid(conv)

    gates = gt_ref[0]
    beta_t = _sigmoid(gates)
    g_t = -jnp.exp(alog_ref[...]) * _softplus(gates + dtb_ref[...])
    rr = lax.broadcasted_iota(jnp.int32, (rows, rows), 0)
    cc = lax.broadcasted_iota(jnp.int32, (rows, rows), 1)
    same_chunk = (rr // L) == (cc // L)
    gc_t = _dot_exact_lhs((cc <= rr) & same_chunk, g_t)
    egc_t = jnp.exp(gc_t)
    gc_tt = jnp.transpose(gc_t)
    row = lax.broadcasted_iota(jnp.int32, (L, L), 0)
    col = lax.broadcasted_iota(jnp.int32, (L, L), 1)
    tri = col <= row
    strict = col < row
    eye = (col == row).astype(F32)
    z_all = z_ref[0]
    ng = ng_ref[...]

    inst = [(c, h) for c in range(nc) for h in range(H_B)]

    def part(c, h, p):
        return act[c * L:(c + 1) * L, p * D_B + h * DH_B:p * D_B + (h + 1) * DH_B]

    def col(tile, c, lane0, h):
        return tile[c * L:(c + 1) * L, lane0 + h:lane0 + h + 1]

    hr = lax.broadcasted_iota(jnp.int32, (D_B, D_B), 0) // DH_B
    hc = lax.broadcasted_iota(jnp.int32, (D_B, D_B), 1) // DH_B
    head_ones = (hr == hc).astype(BF16)

    def head_sumsq(x):
        hi, lo = _split(x * x)
        return (jnp.dot(hi, head_ones, preferred_element_type=F32)
                + jnp.dot(lo, head_ones, preferred_element_type=F32))

    q_all, k_all = act[:, 0:D_B], act[:, D_B:2 * D_B]
    qn_all = q_all * (lax.rsqrt(head_sumsq(q_all) + EPS) * (DH_B ** -0.5))
    kn_all = k_all * lax.rsqrt(head_sumsq(k_all) + EPS)
    qn = [qn_all[c * L:(c + 1) * L, h * DH_B:(h + 1) * DH_B] for c, h in inst]
    kn = [kn_all[c * L:(c + 1) * L, h * DH_B:(h + 1) * DH_B] for c, h in inst]
    betas = [col(beta_t, c, G_BETA, h) for c, h in inst]
    gcs = [col(gc_t, c, G_A, h) for c, h in inst]
    egcs = [col(egc_t, c, G_A, h) for c, h in inst]
    gams = [jnp.exp(jnp.where(tri, gc - gc_tt[G_A + h:G_A + h + 1, c * L:(c + 1) * L], -jnp.inf))
            for (c, h), gc in zip(inst, gcs)]
    gls = [gc[L - 1:L, :] for gc in gcs]
    kks = [_dot_nt(k, k) for k in kn]
    nn = [-jnp.where(strict, beta * kk * gam, 0.0) for beta, kk, gam in zip(betas, kks, gams)]
    x_inv = [eye + n for n in nn]
    pw = nn
    for _ in range(5):
        pw = [_dot(p, p) for p in pw]
        x_inv = [xi + _dot(xi, p) for xi, p in zip(x_inv, pw)]
    us = [_dot(xi, beta * part(c, h, 2)) for (c, h), xi, beta in zip(inst, x_inv, betas)]
    ws = [_dot(xi, (beta * egc) * k) for xi, beta, egc, k in zip(x_inv, betas, egcs, kn)]
    qks = [jnp.where(tri, _dot_nt(q, k) * gam, 0.0) for q, k, gam in zip(qn, kn, gams)]
    qes = [q * egc for q, egc in zip(qn, egcs)]
    kdts = [jnp.transpose(k * jnp.exp(gl - gc)) for k, gl, gc in zip(kn, gls, gcs)]
    egls = [jnp.exp(gl) for gl in gls]
    kdw = [_dot(kdt, w) for kdt, w in zip(kdts, ws)]
    drive = [_dot(kdt, u) for kdt, u in zip(kdts, us)]

    states = [None] * len(inst)
    state = [s_scr[h] for h in range(H_B)]
    for c in range(nc):
        for h in range(H_B):
            states[c * H_B + h] = state[h]
        state = [state[h] * egls[c * H_B + h] - _dot(kdw[c * H_B + h], state[h]) + drive[c * H_B + h]
                 for h in range(H_B)]
    v_new = [u - _dot(w, s) for u, w, s in zip(us, ws, states)]
    outs = [_dot(qe, s) + _dot(qk, vn) for qe, s, qk, vn in zip(qes, states, qks, v_new)]
    o_ms = [jnp.mean(o * o, axis=-1, keepdims=True) for o in outs]
    o_n = [o * lax.rsqrt(ms + EPS) * ng for o, ms in zip(outs, o_ms)]
    for c in range(nc):
        zc = z_all[c * L:(c + 1) * L, :]
        o_ref[0, c * L:(c + 1) * L, :] = (
            jnp.concatenate(o_n[c * H_B:(c + 1) * H_B], axis=-1) * (zc * _sigmoid(zc)))
    for h in range(H_B):
        s_scr[h] = state[h]
    so_ref[0] = s_scr[...]


def _gdn(qkv, z, gates, hist, s0, conv_w, alog_row, dtb_row, ng_row):
    b, t, _ = qkv.shape
    nc = min(GDN_CHUNKS, t // CHUNK)
    rows = nc * CHUNK
    blk = lambda bb, c: (bb, c, 0)
    per_b3 = lambda bb, c: (bb, 0, 0)
    per_b4 = lambda bb, c: (bb, 0, 0, 0)
    fixed = lambda bb, c: (0, 0)
    return pl.pallas_call(
        functools.partial(_gdn_kernel, nc=nc),
        grid=(b, t // rows),
        in_specs=[pl.BlockSpec((1, rows, 3 * D_B), blk),
                  pl.BlockSpec((1, rows, D_B), blk),
                  pl.BlockSpec((1, rows, LANES), blk),
                  pl.BlockSpec((1, GDN_CONV - 1, 3 * D_B), per_b3),
                  pl.BlockSpec((1, H_B, DH_B, DH_B), per_b4),
                  pl.BlockSpec((GDN_CONV, 3 * D_B), fixed),
                  pl.BlockSpec((1, LANES), fixed),
                  pl.BlockSpec((1, LANES), fixed),
                  pl.BlockSpec((1, DH_B), fixed)],
        out_specs=[pl.BlockSpec((1, rows, D_B), blk),
                   pl.BlockSpec((1, GDN_CONV - 1, 3 * D_B), per_b3),
                   pl.BlockSpec((1, H_B, DH_B, DH_B), per_b4)],
        out_shape=[jax.ShapeDtypeStruct((b, t, D_B), F32),
                   jax.ShapeDtypeStruct((b, GDN_CONV - 1, 3 * D_B), F32),
                   jax.ShapeDtypeStruct((b, H_B, DH_B, DH_B), F32)],
        scratch_shapes=[pltpu.VMEM((SUBLANES + rows, 3 * D_B), F32),
                        pltpu.VMEM((H_B, DH_B, DH_B), F32)],
        compiler_params=_params(("parallel", "arbitrary")),
        name="gdn",
    )(qkv, z, gates, hist, s0, conv_w, alog_row, dtb_row, ng_row)


def _mlstm_kernel(x_ref, gt_ref, c0_ref, n0_ref, m0_ref, bi_ref, bf_ref, ng_ref,
                  o_ref, co_ref, no_ref, mo_ref, c_scr, n_scr, m_scr, *, nc):
    L = CHUNK
    rows = nc * L
    step = pl.program_id(1)

    @pl.when(step == 0)
    def _():
        c_scr[...] = c0_ref[0]
        n_scr[...] = n0_ref[0]
        m_scr[...] = m0_ref[0]

    gates = gt_ref[0]
    ig_t = gates + bi_ref[...]
    xf = gates + bf_ref[...]
    lf_t = jnp.minimum(xf, 0.0) - jnp.log(1.0 + jnp.exp(-jnp.abs(xf)))
    rr = lax.broadcasted_iota(jnp.int32, (rows, rows), 0)
    cc = lax.broadcasted_iota(jnp.int32, (rows, rows), 1)
    same_chunk = (rr // L) == (cc // L)
    bc_t = _dot_exact_lhs((cc <= rr) & same_chunk, lf_t)
    bc_tt = jnp.transpose(bc_t)
    ig_tt = jnp.transpose(ig_t)
    row = lax.broadcasted_iota(jnp.int32, (L, L), 0)
    col = lax.broadcasted_iota(jnp.int32, (L, L), 1)
    tri = col <= row
    lane = lax.broadcasted_iota(jnp.int32, (1, LANES), 1)
    m_all = m_scr[...]
    ng = ng_ref[...]

    inst = [(c, h) for c in range(nc) for h in range(H_C)]

    def sl(c, h, part):
        return x_ref[0, c * L:(c + 1) * L, part * D_C + h * DH_C:part * D_C + (h + 1) * DH_C]

    bcs = [bc_t[c * L:(c + 1) * L, G_F + h:G_F + h + 1] for c, h in inst]
    igs = [ig_t[c * L:(c + 1) * L, G_I + h:G_I + h + 1] for c, h in inst]
    dmats = [jnp.where(tri, bc - bc_tt[G_F + h:G_F + h + 1, c * L:(c + 1) * L]
                       + ig_tt[G_I + h:G_I + h + 1, c * L:(c + 1) * L], -jnp.inf)
             for (c, h), bc in zip(inst, bcs)]
    dmaxs = [jnp.max(d, axis=-1, keepdims=True) for d in dmats]
    ks = [sl(c, h, 1) * (DH_C ** -0.5) for c, h in inst]
    qk_raw = [_dot_nt(sl(c, h, 0), k) for (c, h), k in zip(inst, ks)]

    gs, ms = [None] * len(inst), [None] * len(inst)
    for h in range(H_C):
        m0 = m_all[:, h:h + 1]
        for c in range(nc):
            i = c * H_C + h
            gs[i] = bcs[i] + m0
            ms[i] = jnp.maximum(gs[i], dmaxs[i])
            m0 = ms[i][L - 1:L, :]
        m_all = jnp.where(lane == h, m0, m_all)
    m_last = [m[L - 1:L, :] for m in ms]
    decays = [jnp.exp(g[L - 1:L, :] - ml) for g, ml in zip(gs, m_last)]
    kds = [k * jnp.exp(bc[L - 1:L, :] - bc + ig - ml) for k, bc, ig, ml in zip(ks, bcs, igs, m_last)]
    kvs = [_dot(jnp.transpose(kd), sl(c, h, 2)) for (c, h), kd in zip(inst, kds)]
    ksums = [jnp.sum(kd, axis=0, keepdims=True) for kd in kds]

    cms, nrows = [None] * len(inst), [None] * len(inst)
    for h in range(H_C):
        cm, nrow = c_scr[h], n_scr[h:h + 1, :]
        for c in range(nc):
            i = c * H_C + h
            cms[i], nrows[i] = cm, nrow
            cm = decays[i] * cm + kvs[i]
            nrow = decays[i] * nrow + ksums[i]
        c_scr[h] = cm
        n_scr[h:h + 1, :] = nrow

    inters = [jnp.exp(g - m) for g, m in zip(gs, ms)]
    qks = [r * jnp.exp(d - m) for r, d, m in zip(qk_raw, dmats, ms)]
    nums = [it * _dot(sl(c, h, 0), cm) + _dot(qk, sl(c, h, 2))
            for (c, h), it, cm, qk in zip(inst, inters, cms, qks)]
    qns = [jnp.sum(sl(c, h, 0) * nrow, axis=-1, keepdims=True) for (c, h), nrow in zip(inst, nrows)]
    qksums = [jnp.sum(qk, axis=-1, keepdims=True) for qk in qks]
    dens = [it * qn + s for it, qn, s in zip(inters, qns, qksums)]
    hhs = [num / jnp.maximum(jnp.abs(den), jnp.exp(-m)) for num, den, m in zip(nums, dens, ms)]
    mss = [jnp.mean(hh * hh, axis=-1, keepdims=True) for hh in hhs]
    for (c, h), hh, msq in zip(inst, hhs, mss):
        o_ref[0, c * L:(c + 1) * L, h * DH_C:(h + 1) * DH_C] = (
            hh * lax.rsqrt(msq + EPS) * ng * _sigmoid(sl(c, h, 3)))
    m_scr[...] = m_all
    co_ref[0] = c_scr[...]
    no_ref[0] = n_scr[...]
    mo_ref[0] = m_all


def _mlstm(x, gates, c0, n0, m0, bi_row, bf_row, ng_row):
    b, t, _ = x.shape
    nc = min(MLSTM_CHUNKS, t // CHUNK)
    rows = nc * CHUNK
    blk = lambda bb, c: (bb, c, 0)
    per_b3 = lambda bb, c: (bb, 0, 0)
    per_b4 = lambda bb, c: (bb, 0, 0, 0)
    fixed = lambda bb, c: (0, 0)
    return pl.pallas_call(
        functools.partial(_mlstm_kernel, nc=nc),
        grid=(b, t // rows),
        in_specs=[pl.BlockSpec((1, rows, 4 * D_C), blk),
                  pl.BlockSpec((1, rows, LANES), blk),
                  pl.BlockSpec((1, H_C, DH_C, DH_C), per_b4),
                  pl.BlockSpec((1, H_C, DH_C), per_b3),
                  pl.BlockSpec((1, 1, LANES), per_b3),
                  pl.BlockSpec((1, LANES), fixed),
                  pl.BlockSpec((1, LANES), fixed),
                  pl.BlockSpec((1, DH_C), fixed)],
        out_specs=[pl.BlockSpec((1, rows, D_C), blk),
                   pl.BlockSpec((1, H_C, DH_C, DH_C), per_b4),
                   pl.BlockSpec((1, H_C, DH_C), per_b3),
                   pl.BlockSpec((1, 1, LANES), per_b3)],
        out_shape=[jax.ShapeDtypeStruct((b, t, D_C), F32),
                   jax.ShapeDtypeStruct((b, H_C, DH_C, DH_C), F32),
                   jax.ShapeDtypeStruct((b, H_C, DH_C), F32),
                   jax.ShapeDtypeStruct((b, 1, LANES), F32)],
        scratch_shapes=[pltpu.VMEM((H_C, DH_C, DH_C), F32),
                        pltpu.VMEM((H_C, DH_C), F32),
                        pltpu.VMEM((1, LANES), F32)],
        compiler_params=_params(("parallel", "arbitrary")),
        name="mlstm",
    )(x, gates, c0, n0, m0, bi_row, bf_row, ng_row)


def _mix_ffn_kernel(x_ref, a_ref, b_ref, c_ref, wo_ref, g0_ref, hist_ref, g1_ref, wu_ref, cw_ref,
                    wd_ref, g2_ref, o_ref, histo_ref, gp_scr, hist_scr, acc_scr, *, rows):
    t = pl.program_id(1)
    h0 = SUBLANES - (FFN_CONV - 1)

    @pl.when(t == 0)
    def _():
        hist_scr[h0:SUBLANES, :] = hist_ref[0]

    oa = jnp.concatenate([a_ref[hh] for hh in range(H_A)], axis=-1)
    mix = (_dot(oa, wo_ref[0:D_A, :])
           + _dot(b_ref[...], wo_ref[D_A:D_A + D_B, :])
           + _dot(c_ref[...], wo_ref[D_A + D_B:, :]))
    x = x_ref[0] + _rms(mix, g0_ref[...])
    h = _rms(x, g1_ref[...]).astype(BF16)
    acc_scr[...] = jnp.zeros_like(acc_scr)
    for ci in range(D_FF // FF_CHUNK):
        lo, hi = ci * FF_CHUNK, (ci + 1) * FF_CHUNK
        gate = jnp.dot(h, wu_ref[:, lo:hi], preferred_element_type=F32)
        up = jnp.dot(h, wu_ref[:, D_FF + lo:D_FF + hi], preferred_element_type=F32)
        gp_scr[h0:SUBLANES, :] = hist_scr[h0:SUBLANES, lo:hi]
        gp_scr[SUBLANES:SUBLANES + rows, :] = gate
        cw = cw_ref[:, lo:hi]
        conv = gate * cw[FFN_CONV - 1:FFN_CONV, :]
        for j in range(FFN_CONV - 1):
            conv = conv + gp_scr[h0 + j:h0 + j + rows, :] * cw[j:j + 1, :]
        hist_scr[h0:SUBLANES, lo:hi] = gp_scr[rows + h0:rows + SUBLANES, :]
        act = jax.nn.gelu(conv, approximate=True) * up
        acc_scr[...] += jnp.dot(act.astype(BF16), wd_ref[lo:hi, :], preferred_element_type=F32)
    histo_ref[0] = hist_scr[h0:SUBLANES, :]
    o_ref[0] = x + _rms(acc_scr[...], g2_ref[...])


def _mix_ffn(x, oa, ob, oc, w_out, g0, hist, g1, w_up, conv_w, w_down, g2):
    b, t, _ = x.shape
    r = min(ROW_TILE, t)
    assert t % r == 0
    nt = t // r
    blk = lambda bb, i: (bb, i, 0)
    flat = lambda bb, i: (bb * nt + i, 0)
    per_b = lambda bb, i: (bb, 0, 0)
    fixed = lambda bb, i: (0, 0)
    once = pl.Buffered(1)
    kern = functools.partial(_mix_ffn_kernel, rows=r)
    return pl.pallas_call(
        kern,
        grid=(b, nt),
        in_specs=[pl.BlockSpec((1, r, D_MODEL), blk),
                  pl.BlockSpec((H_A, r, DH_A), lambda bb, i: (0, bb * nt + i, 0)),
                  pl.BlockSpec((r, D_B), flat),
                  pl.BlockSpec((r, D_C), flat),
                  pl.BlockSpec(w_out.shape, fixed, pipeline_mode=once),
                  pl.BlockSpec((1, D_MODEL), fixed),
                  pl.BlockSpec((1, FFN_CONV - 1, D_FF), per_b),
                  pl.BlockSpec((1, D_MODEL), fixed),
                  pl.BlockSpec(w_up.shape, fixed, pipeline_mode=once),
                  pl.BlockSpec((FFN_CONV, D_FF), fixed),
                  pl.BlockSpec(w_down.shape, fixed, pipeline_mode=once),
                  pl.BlockSpec((1, D_MODEL), fixed)],
        out_specs=[pl.BlockSpec((1, r, D_MODEL), blk),
                   pl.BlockSpec((1, FFN_CONV - 1, D_FF), per_b)],
        out_shape=[jax.ShapeDtypeStruct((b, t, D_MODEL), F32),
                   jax.ShapeDtypeStruct((b, FFN_CONV - 1, D_FF), F32)],
        scratch_shapes=[pltpu.VMEM((SUBLANES + r, FF_CHUNK), F32),
                        pltpu.VMEM((SUBLANES, D_FF), F32),
                        pltpu.VMEM((r, D_MODEL), F32)],
        compiler_params=_params(("parallel", "arbitrary")),
        name="mix_ffn",
    )(x, oa, ob, oc, w_out, g0, hist, g1, w_up, conv_w, w_down, g2)


def _lane_row(vals, offset):
    return jnp.zeros((1, LANES), F32).at[0, offset:offset + vals.shape[0]].set(vals.astype(F32))


def _prep_weights(w_in, w_out, ffn_w_up, ffn_w_down):
    c_gate0 = 3 * D_A + 4 * D_B
    c_c0 = c_gate0 + 2 * H_B
    c_gate1 = c_c0 + 4 * D_C
    wb = w_in.astype(BF16)
    w_main = jnp.concatenate([wb[:, :c_gate0], wb[:, c_c0:c_gate1]], axis=1)
    n_gate = 2 * H_B + 2 * H_C
    w_gate = jnp.concatenate([wb[:, c_gate0:c_c0], wb[:, c_gate1:],
                              jnp.zeros((D_MODEL, LANES - n_gate), BF16)], axis=1)
    return w_main, w_gate, w_out.astype(BF16), ffn_w_up.astype(BF16), ffn_w_down.astype(BF16)


def _layer(x, kv_k, kv_v, gdn_hist, gdn_s, m_c, m_n, m_m, ffn_hist, lw):
    (g_mix_pre, g_mix_post, g_ffn_pre, g_ffn_post, w_main, w_gate, gdn_conv_w, alog_row, dtb_row,
     gdn_ng, bi_row, bf_row, mlstm_ng, w_out, w_up, ffn_conv_w, w_down) = lw
    b, t, _ = x.shape
    n = b * t
    xf = x.reshape(n, D_MODEL)
    q_hm, kt_hm, v_hm, k_f, v_f, qkv_b, z_b, qkvo_c, gates = _inproj(xf, g_mix_pre, w_main, w_gate)

    def streams_first(a):
        return a.reshape(H_A, b, t, DH_A).transpose(1, 0, 2, 3)

    k_new, v_new = streams_first(k_f), streams_first(v_f)
    g = b * H_A
    q_g = streams_first(q_hm).reshape(g, t, DH_A)
    if kv_k is None:
        pad = -(-t // ATTN_TK) * ATTN_TK - t
        kt_all = kt_hm.reshape(H_A, DH_A, b, t).transpose(2, 0, 1, 3)
        v_all = streams_first(v_hm)
        if pad > 0:
            kt_all = jnp.pad(kt_all, ((0, 0), (0, 0), (0, 0), (0, pad)))
            v_all = jnp.pad(v_all, ((0, 0), (0, 0), (0, pad), (0, 0)))
        o_a = _attention(q_g, kt_all.reshape(g, DH_A, t + pad), v_all.reshape(g, t + pad, DH_A))
    else:
        past = kv_k.shape[2]
        o_a = _attention_cached(q_g, kv_k.reshape(g, past, DH_A), kv_v.reshape(g, past, DH_A),
                                k_new.astype(BF16).reshape(g, t, DH_A),
                                streams_first(v_hm).reshape(g, t, DH_A))
    o_a = o_a.reshape(b, H_A, t, DH_A).transpose(1, 0, 2, 3).reshape(H_A, n, DH_A)

    o_b, gdn_hist_new, s_new = _gdn(qkv_b.reshape(b, t, 3 * D_B), z_b.reshape(b, t, D_B),
                                    gates.reshape(b, t, LANES), gdn_hist, gdn_s, gdn_conv_w,
                                    alog_row, dtb_row, gdn_ng)

    o_c, c_new, n_new, m_new = _mlstm(qkvo_c.reshape(b, t, 4 * D_C), gates.reshape(b, t, LANES),
                                      m_c, m_n, m_m, bi_row, bf_row, mlstm_ng)

    x2, ffn_hist_new = _mix_ffn(x, o_a, o_b.reshape(n, D_B), o_c.reshape(n, D_C), w_out, g_mix_post,
                                ffn_hist, g_ffn_pre, w_up, ffn_conv_w, w_down, g_ffn_post)
    return (x2, k_new, v_new, gdn_hist_new, s_new, c_new, n_new, m_new[:, 0, :H_C], ffn_hist_new)


def kernel(x_prompt, x_sample, cache_sb_k, cache_sb_v, state_gdn_conv, state_gdn_s, state_mlstm_c, state_mlstm_n, state_mlstm_m, state_ffn_conv, g_mix_pre, g_mix_post, g_ffn_pre, g_ffn_post, w_in, gdn_conv_w, gdn_a_log, gdn_dt_bias, gdn_norm_g, mlstm_b_i, mlstm_b_f, mlstm_norm_g, w_out, ffn_w_up, ffn_conv_w, ffn_w_down):
    depth = w_in.shape[0]
    bp = x_prompt.shape[0]
    bs = x_sample.shape[0]
    zero_gdn_hist = jnp.zeros((bp, GDN_CONV - 1, 3 * D_B), F32)
    zero_s = jnp.zeros((bp, H_B, DH_B, DH_B), F32)
    zero_c = jnp.zeros((bp, H_C, DH_C, DH_C), F32)
    zero_n = jnp.zeros((bp, H_C, DH_C), F32)
    m_init = jnp.full((bp, 1, LANES), NEG, F32)
    zero_ffn_hist = jnp.zeros((bp, FFN_CONV - 1, D_FF), F32)

    xp, xs = x_prompt, x_sample
    new_p, new_s = [], []
    for l in range(depth):
        w_main, w_gate, w_o, w_u, w_d = _prep_weights(w_in[l], w_out[l], ffn_w_up[l], ffn_w_down[l])
        lw = (g_mix_pre[l][None], g_mix_post[l][None], g_ffn_pre[l][None], g_ffn_post[l][None],
              w_main, w_gate, gdn_conv_w[l], _lane_row(gdn_a_log[l], G_A),
              _lane_row(gdn_dt_bias[l], G_A), gdn_norm_g[l][None],
              _lane_row(mlstm_b_i[l], G_I), _lane_row(mlstm_b_f[l], G_F), mlstm_norm_g[l][None],
              w_o, w_u, ffn_conv_w[l], w_d)
        xp, *st_p = _layer(xp, None, None, zero_gdn_hist, zero_s, zero_c, zero_n, m_init,
                           zero_ffn_hist, lw)
        m_s = jnp.zeros((bs, 1, LANES), F32).at[:, 0, :H_C].set(state_mlstm_m[l])
        xs, *st_s = _layer(xs, cache_sb_k[l], cache_sb_v[l], state_gdn_conv[l], state_gdn_s[l],
                           state_mlstm_c[l], state_mlstm_n[l], m_s, state_ffn_conv[l], lw)
        new_p.append(st_p)
        new_s.append(st_s)
    outs_p = [jnp.stack(a) for a in zip(*new_p)]
    outs_s = [jnp.stack(a) for a in zip(*new_s)]
    return (xp, xs, *outs_p, *outs_s)
```

```python
import functools

import jax
import jax.numpy as jnp
import numpy as np
from jax import lax
from jax.experimental import pallas as pl
from jax.experimental.pallas import tpu as pltpu

F32 = jnp.float32
BF16 = jnp.bfloat16

D_MODEL = 1024
CHUNK = 64
H_A, DH_A = 4, 64
H_B, DH_B = 4, 64
H_C, DH_C = 4, 128
D_A, D_B, D_C = H_A * DH_A, H_B * DH_B, H_C * DH_C
GDN_CONV = 4
D_FF = 2816
FFN_CONV = 3
EPS = 1e-6
NEG = -1e30

LANES = 128
SUBLANES = 8
ROW_TILE = 512
ATTN_TQ = 1024
ATTN_TK = 256
ATTN_UNROLL = 4
FF_CHUNK = 2816
GDN_CHUNKS = 4
MLSTM_CHUNKS = 4
VMEM_LIMIT = 56 * 1024 * 1024

G_BETA, G_A, G_I, G_F = 0, 4, 8, 12


def _params(sem):
    return pltpu.CompilerParams(dimension_semantics=sem, vmem_limit_bytes=VMEM_LIMIT)


def _dot(a, b):
    return jnp.dot(a.astype(BF16), b.astype(BF16), preferred_element_type=F32)


def _dot_nt(a, b):
    return lax.dot_general(a.astype(BF16), b.astype(BF16), (((1,), (1,)), ((), ())),
                           preferred_element_type=F32)


def _split(a):
    hi = a.astype(BF16)
    lo = (a - hi.astype(F32)).astype(BF16)
    return hi, lo


def _dot3(a, b):
    ah, al = _split(a)
    bh, bl = _split(b)
    return (jnp.dot(ah, bh, preferred_element_type=F32)
            + jnp.dot(ah, bl, preferred_element_type=F32)
            + jnp.dot(al, bh, preferred_element_type=F32))


def _dot_exact_lhs(a01, b):
    a = a01.astype(BF16)
    b0 = b.astype(BF16)
    r1 = b - b0.astype(F32)
    b1 = r1.astype(BF16)
    b2 = (r1 - b1.astype(F32)).astype(BF16)
    return (jnp.dot(a, b0, preferred_element_type=F32)
            + jnp.dot(a, b1, preferred_element_type=F32)
            + jnp.dot(a, b2, preferred_element_type=F32))


def _sigmoid(x):
    return 1.0 / (1.0 + jnp.exp(-x))


def _softplus(x):
    return jnp.maximum(x, 0.0) + jnp.log(1.0 + jnp.exp(-jnp.abs(x)))


def _rms(x, g):
    return x * lax.rsqrt(jnp.mean(x * x, axis=-1, keepdims=True) + EPS) * g


def _inproj_kernel(x_ref, g_ref, w_ref, wg_ref, q_ref, kt_ref, v_ref, kf_ref, vf_ref,
                   b_ref, z_ref, c_ref, gt_ref):
    h = _rms(x_ref[...], g_ref[...]).astype(BF16)
    qkv = jnp.dot(h, w_ref[:, 0:3 * D_A], preferred_element_type=F32)
    for hh in range(H_A):
        lo = hh * DH_A
        q_ref[hh] = qkv[:, lo:lo + DH_A].astype(BF16)
        kf_ref[hh] = qkv[:, D_A + lo:D_A + lo + DH_A]
        v_h = qkv[:, 2 * D_A + lo:2 * D_A + lo + DH_A]
        vf_ref[hh] = v_h
        v_ref[hh] = v_h.astype(BF16)
    for pair in range(H_A * DH_A // LANES):
        kt_pair = jnp.transpose(qkv[:, D_A + pair * LANES:D_A + (pair + 1) * LANES])
        for sub in range(LANES // DH_A):
            kt_ref[pair * (LANES // DH_A) + sub] = kt_pair[sub * DH_A:(sub + 1) * DH_A, :].astype(BF16)
    b_ref[...] = jnp.dot(h, w_ref[:, 768:1536], preferred_element_type=F32)
    z_ref[...] = jnp.dot(h, w_ref[:, 1536:1792], preferred_element_type=F32)
    c_ref[...] = jnp.dot(h, w_ref[:, 1792:3840], preferred_element_type=F32)
    gt_ref[...] = jnp.dot(h, wg_ref[...], preferred_element_type=F32)


def _inproj(x, g, w_main, w_gate):
    n = x.shape[0]
    r = min(ROW_TILE, n)
    assert n % r == 0
    widths = (3 * D_B, D_B, 4 * D_C, LANES)
    row = lambda i: (i, 0)
    fixed = lambda i: (0, 0)
    head_rows = lambda i: (0, i, 0)
    head_cols = lambda i: (0, 0, i)
    hm_spec = pl.BlockSpec((H_A, r, DH_A), head_rows)
    hm = lambda dt: jax.ShapeDtypeStruct((H_A, n, DH_A), dt)
    return pl.pallas_call(
        _inproj_kernel,
        grid=(n // r,),
        in_specs=[pl.BlockSpec((r, D_MODEL), row),
                  pl.BlockSpec((1, D_MODEL), fixed),
                  pl.BlockSpec(w_main.shape, fixed),
                  pl.BlockSpec(w_gate.shape, fixed)],
        out_specs=[hm_spec, pl.BlockSpec((H_A, DH_A, r), head_cols), hm_spec, hm_spec, hm_spec]
                  + [pl.BlockSpec((r, w), row) for w in widths],
        out_shape=[hm(BF16), jax.ShapeDtypeStruct((H_A, DH_A, n), BF16), hm(BF16), hm(F32), hm(F32)]
                  + [jax.ShapeDtypeStruct((n, w), F32) for w in widths],
        compiler_params=_params(("parallel",)),
        name="inproj",
    )(x, g, w_main, w_gate)


def _attn_body(q_ref, u_ref, o_ref, qs_scr, z_scr, sp_scr, a_scr, acc_scr, car_scr,
               qk_block, v_block, n_blk, *, tq, tk, unroll, q_start):
    i = pl.program_id(1)
    q_pos0 = q_start + i * tq
    n_all = (q_pos0 + tq - 1 + tk - 1) // tk
    n_full = q_pos0 // tk
    qs_scr[...] = q_ref[0] * jnp.asarray(DH_A ** -0.5, BF16)
    acc_scr[...] = jnp.zeros_like(acc_scr)
    car_scr[...] = jnp.zeros_like(car_scr)
    z_scr[1] = jnp.full((tq, tk), NEG, BF16)
    sp_scr[1] = jnp.zeros((tq, tk), BF16)
    a_scr[0] = jnp.zeros((tq, tk), BF16)

    def stage_qk(j, masked, slot, r0=0):
        ks = pl.multiple_of(jnp.clip(j, 0, n_blk - 1) * tk, tk)
        z = qk_block(qs_scr[r0:, :], ks)
        if masked:
            kpos = j * tk + lax.broadcasted_iota(jnp.int32, (tq - r0, tk), 1)
            qpos = q_pos0 + r0 + lax.broadcasted_iota(jnp.int32, (tq - r0, tk), 0)
            z = jnp.where(kpos < qpos, z, NEG)
        zb = z.astype(BF16)
        z_scr[slot, r0:, :] = zb
        sp_scr[slot, r0:, :] = _softplus(zb)

    def stage_exp(slot, r0=0):
        mm = jnp.dot(sp_scr[slot, r0:, :], u_ref[...], preferred_element_type=F32)
        car = car_scr[r0:, :]
        rest = mm + jnp.concatenate([car] * (tk // LANES), axis=1)
        a_scr[slot, r0:, :] = jnp.exp(z_scr[slot, r0:, :] - rest.astype(BF16))
        car_scr[r0:, :] = car + jnp.broadcast_to(mm[:, 0:1], (tq - r0, LANES))

    def stage_pv(j, slot, r0=0):
        ks = pl.multiple_of(jnp.clip(j, 0, n_blk - 1) * tk, tk)
        acc_scr[r0:, :] += jnp.dot(a_scr[slot, r0:, :], v_block(ks), preferred_element_type=F32)

    def step(j, masked, slot):
        stage_qk(j, masked, slot)
        stage_exp(1 - slot)
        stage_pv(j + 2, slot)

    n_diag_static = tq // tk
    if q_start % (unroll * tk) == 0 and tq % tk == 0 and n_diag_static == unroll:
        n_full_u = n_full
        for u in range(unroll):
            d = unroll - 1 - u
            stage_qk(n_full + d, True, u % 2, d * tk)
            if d + 1 < unroll:
                stage_exp(1 - u % 2, (d + 1) * tk)
            if d + 2 < unroll:
                stage_pv(n_full + d + 2, u % 2, (d + 2) * tk)
    else:
        n_full_u = (n_full // unroll) * unroll
        n_diag_u = -(-(n_all - n_full_u) // unroll) * unroll

        def diag_body(t, c):
            j = n_full_u + n_diag_u - 1 - unroll * t
            for u in range(unroll):
                step(j - u, True, u % 2)
            return c

        lax.fori_loop(0, n_diag_u // unroll, diag_body, 0)

    def full_body(t, c):
        j = n_full_u - 1 - unroll * t
        for u in range(unroll):
            step(j - u, False, u % 2)
        return c

    lax.fori_loop(0, n_full_u // unroll, full_body, 0)
    stage_exp(1)
    stage_pv(1, 0)
    stage_pv(0, 1)
    o_ref[0] = acc_scr[...]


def _attn_kernel(q_ref, kt_ref, v_ref, u_ref, o_ref, *scratch, tk, **static):
    def qk_block(q, ks):
        return jnp.dot(q, kt_ref[0, :, pl.ds(ks, tk)], preferred_element_type=F32)

    def v_block(ks):
        return v_ref[0, pl.ds(ks, tk), :]

    _attn_body(q_ref, u_ref, o_ref, *scratch, qk_block, v_block, v_ref.shape[1] // tk, tk=tk, **static)


def _attn_cached_kernel(q_ref, kp_ref, vp_ref, kn_ref, vn_ref, u_ref, o_ref, *scratch, tk, **static):
    *pipe_scratch, k_scr, v_scr = scratch
    past, t = kp_ref.shape[1], kn_ref.shape[1]
    for dst, cached, new in ((k_scr, kp_ref, kn_ref), (v_scr, vp_ref, vn_ref)):
        dst[0:past, :] = cached[0].astype(BF16)
        dst[past:past + t, :] = new[0]
        dst[past + t:, :] = jnp.zeros((dst.shape[0] - past - t, dst.shape[1]), BF16)

    def qk_block(q, ks):
        return lax.dot_general(q, k_scr[pl.ds(ks, tk), :], (((1,), (1,)), ((), ())),
                               preferred_element_type=F32)

    def v_block(ks):
        return v_scr[pl.ds(ks, tk), :]

    _attn_body(q_ref, u_ref, o_ref, *pipe_scratch, qk_block, v_block, k_scr.shape[0] // tk, tk=tk,
               **static)


def _attn_call(kern, g, t, tq, tk, d, in_arrays, in_specs, extra_scratch):
    u = jnp.asarray((np.arange(tk)[:, None] >= np.arange(tk)[None, :]).astype(np.float32), BF16)
    return pl.pallas_call(
        kern,
        grid=(g, t // tq),
        in_specs=[pl.BlockSpec((1, tq, d), lambda h, i: (h, i, 0))] + in_specs
                 + [pl.BlockSpec(u.shape, lambda h, i: (0, 0))],
        out_specs=pl.BlockSpec((1, tq, d), lambda h, i: (h, i, 0)),
        out_shape=jax.ShapeDtypeStruct((g, t, d), F32),
        scratch_shapes=[pltpu.VMEM((tq, d), BF16), pltpu.VMEM((2, tq, tk), BF16),
                        pltpu.VMEM((2, tq, tk), BF16), pltpu.VMEM((2, tq, tk), BF16),
                        pltpu.VMEM((tq, d), F32), pltpu.VMEM((tq, LANES), F32)] + extra_scratch,
        compiler_params=_params(("parallel", "arbitrary")),
        name="attn",
    )(*in_arrays, u)


def _attn_static(t, q_start):
    tq, tk = min(ATTN_TQ, t), ATTN_TK
    assert t % tq == 0
    unroll = ATTN_UNROLL if tq >= ATTN_UNROLL * tk else 2
    return tq, tk, dict(tq=tq, tk=tk, unroll=unroll, q_start=q_start)


def _attention(q, kt, v):
    g, t, d = q.shape
    tk_total = v.shape[1]
    tq, tk, static = _attn_static(t, 0)
    whole = lambda h, i: (h, 0, 0)
    return _attn_call(functools.partial(_attn_kernel, **static), g, t, tq, tk, d, (q, kt, v),
                      [pl.BlockSpec((1, d, tk_total), whole), pl.BlockSpec((1, tk_total, d), whole)], [])


def _attention_cached(q, k_past, v_past, k_new, v_new):
    g, t, d = q.shape
    past = k_past.shape[1]
    tq, tk, static = _attn_static(t, past)
    tk_total = -(-(past + t) // tk) * tk
    whole = lambda h, i: (h, 0, 0)
    return _attn_call(functools.partial(_attn_cached_kernel, **static), g, t, tq, tk, d,
                      (q, k_past, v_past, k_new, v_new),
                      [pl.BlockSpec((1, past, d), whole), pl.BlockSpec((1, past, d), whole),
                       pl.BlockSpec((1, t, d), whole), pl.BlockSpec((1, t, d), whole)],
                      [pltpu.VMEM((tk_total, d), BF16), pltpu.VMEM((tk_total, d), BF16)])


def _gdn_kernel(x_ref, z_ref, gt_ref, hist_ref, s0_ref, cw_ref, alog_ref, dtb_ref, ng_ref,
                o_ref, histo_ref, so_ref, xp_scr, s_scr, *, nc):
    L = CHUNK
    rows = nc * L
    step = pl.program_id(1)
    h0 = SUBLANES - (GDN_CONV - 1)

    @pl.when(step == 0)
    def _():
        xp_scr[h0:SUBLANES, :] = hist_ref[0]
        s_scr[...] = s0_ref[0]

    x = x_ref[0]
    xp_scr[SUBLANES:SUBLANES + rows, :] = x
    cw = cw_ref[...]
    conv = x * cw[GDN_CONV - 1:GDN_CONV, :]
    xp = xp_scr[...]
    for j in range(GDN_CONV - 1):
        shifted = pltpu.roll(xp, SUBLANES + rows - (h0 + j), axis=0)[:rows, :]
        conv = conv + shifted * cw[j:j + 1, :]
    new_hist = xp_scr[rows + h0:rows + SUBLANES, :]
    xp_scr[h0:SUBLANES, :] = new_hist
    histo_ref[0] = new_hist
    act = conv * _sigmoid(conv)

    gates = gt_ref[0]
    beta_t = _sigmoid(gates)
    g_t = -jnp.exp(alog_ref[...]) * _softplus(gates + dtb_ref[...])
    rr = lax.broadcasted_iota(jnp.int32, (rows, rows), 0)
    cc = lax.broadcasted_iota(jnp.int32, (rows, rows), 1)
    same_chunk = (rr // L) == (cc // L)
    gc_t = _dot_exact_lhs((cc <= rr) & same_chunk, g_t)
    egc_t = jnp.exp(gc_t)
    gc_tt = jnp.transpose(gc_t)
    row = lax.broadcasted_iota(jnp.int32, (L, L), 0)
    col = lax.broadcasted_iota(jnp.int32, (L, L), 1)
    tri = col <= row
    strict = col < row
    eye = (col == row).astype(F32)
    z_all = z_ref[0]
    ng = ng_ref[...]

    inst = [(c, h) for c in range(nc) for h in range(H_B)]

    def part(c, h, p):
        return act[c * L:(c + 1) * L, p * D_B + h * DH_B:p * D_B + (h + 1) * DH_B]

    def col(tile, c, lane0, h):
        return tile[c * L:(c + 1) * L, lane0 + h:lane0 + h + 1]

    hr = lax.broadcasted_iota(jnp.int32, (D_B, D_B), 0) // DH_B
    hc = lax.broadcasted_iota(jnp.int32, (D_B, D_B), 1) // DH_B
    head_ones = (hr == hc).astype(BF16)

    def head_sumsq(x):
        hi, lo = _split(x * x)
        return (jnp.dot(hi, head_ones, preferred_element_type=F32)
                + jnp.dot(lo, head_ones, preferred_element_type=F32))

    q_all, k_all = act[:, 0:D_B], act[:, D_B:2 * D_B]
    qn_all = q_all * (lax.rsqrt(head_sumsq(q_all) + EPS) * (DH_B ** -0.5))
    kn_all = k_all * lax.rsqrt(head_sumsq(k_all) + EPS)
    qn = [qn_all[c * L:(c + 1) * L, h * DH_B:(h + 1) * DH_B] for c, h in inst]
    kn = [kn_all[c * L:(c + 1) * L, h * DH_B:(h + 1) * DH_B] for c, h in inst]
    betas = [col(beta_t, c, G_BETA, h) for c, h in inst]
    gcs = [col(gc_t, c, G_A, h) for c, h in inst]
    egcs = [col(egc_t, c, G_A, h) for c, h in inst]
    gams = [jnp.exp(jnp.where(tri, gc - gc_tt[G_A + h:G_A + h + 1, c * L:(c + 1) * L], -jnp.inf))
            for (c, h), gc in zip(inst, gcs)]
    gls = [gc[L - 1:L, :] for gc in gcs]
    kks = [_dot_nt(k, k) for k in kn]
    nn = [-jnp.where(strict, beta * kk * gam, 0.0) for beta, kk, gam in zip(betas, kks, gams)]
    x_inv = [eye + n for n in nn]
    pw = nn
    for _ in range(5):
        pw = [_dot(p, p) for p in pw]
        x_inv = [xi + _dot(xi, p) for xi, p in zip(x_inv, pw)]
    us = [_dot(xi, beta * part(c, h, 2)) for (c, h), xi, beta in zip(inst, x_inv, betas)]
    ws = [_dot(xi, (beta * egc) * k) for xi, beta, egc, k in zip(x_inv, betas, egcs, kn)]
    qks = [jnp.where(tri, _dot_nt(q, k) * gam, 0.0) for q, k, gam in zip(qn, kn, gams)]
    qes = [q * egc for q, egc in zip(qn, egcs)]
    kdts = [jnp.transpose(k * jnp.exp(gl - gc)) for k, gl, gc in zip(kn, gls, gcs)]
    egls = [jnp.exp(gl) for gl in gls]
    kdw = [_dot(kdt, w) for kdt, w in zip(kdts, ws)]
    drive = [_dot(kdt, u) for kdt, u in zip(kdts, us)]

    states = [None] * len(inst)
    state = [s_scr[h] for h in range(H_B)]
    for c in range(nc):
        for h in range(H_B):
            states[c * H_B + h] = state[h]
        state = [state[h] * egls[c * H_B + h] - _dot(kdw[c * H_B + h], state[h]) + drive[c * H_B + h]
                 for h in range(H_B)]
    v_new = [u - _dot(w, s) for u, w, s in zip(us, ws, states)]
    outs = [_dot(qe, s) + _dot(qk, vn) for qe, s, qk, vn in zip(qes, states, qks, v_new)]
    o_ms = [jnp.mean(o * o, axis=-1, keepdims=True) for o in outs]
    o_n = [o * lax.rsqrt(ms + EPS) * ng for o, ms in zip(outs, o_ms)]
    for c in range(nc):
        zc = z_all[c * L:(c + 1) * L, :]
        o_ref[0, c * L:(c + 1) * L, :] = (
            jnp.concatenate(o_n[c * H_B:(c + 1) * H_B], axis=-1) * (zc * _sigmoid(zc)))
    for h in range(H_B):
        s_scr[h] = state[h]
    so_ref[0] = s_scr[...]


def _gdn(qkv, z, gates, hist, s0, conv_w, alog_row, dtb_row, ng_row):
    b, t, _ = qkv.shape
    nc = min(GDN_CHUNKS, t // CHUNK)
    rows = nc * CHUNK
    blk = lambda bb, c: (bb, c, 0)
    per_b3 = lambda bb, c: (bb, 0, 0)
    per_b4 = lambda bb, c: (bb, 0, 0, 0)
    fixed = lambda bb, c: (0, 0)
    return pl.pallas_call(
        functools.partial(_gdn_kernel, nc=nc),
        grid=(b, t // rows),
        in_specs=[pl.BlockSpec((1, rows, 3 * D_B), blk),
                  pl.BlockSpec((1, rows, D_B), blk),
                  pl.BlockSpec((1, rows, LANES), blk),
                  pl.BlockSpec((1, GDN_CONV - 1, 3 * D_B), per_b3),
                  pl.BlockSpec((1, H_B, DH_B, DH_B), per_b4),
                  pl.BlockSpec((GDN_CONV, 3 * D_B), fixed),
                  pl.BlockSpec((1, LANES), fixed),
                  pl.BlockSpec((1, LANES), fixed),
                  pl.BlockSpec((1, DH_B), fixed)],
        out_specs=[pl.BlockSpec((1, rows, D_B), blk),
                   pl.BlockSpec((1, GDN_CONV - 1, 3 * D_B), per_b3),
                   pl.BlockSpec((1, H_B, DH_B, DH_B), per_b4)],
        out_shape=[jax.ShapeDtypeStruct((b, t, D_B), F32),
                   jax.ShapeDtypeStruct((b, GDN_CONV - 1, 3 * D_B), F32),
                   jax.ShapeDtypeStruct((b, H_B, DH_B, DH_B), F32)],
        scratch_shapes=[pltpu.VMEM((SUBLANES + rows, 3 * D_B), F32),
                        pltpu.VMEM((H_B, DH_B, DH_B), F32)],
        compiler_params=_params(("parallel", "arbitrary")),
        name="gdn",
    )(qkv, z, gates, hist, s0, conv_w, alog_row, dtb_row, ng_row)


def _mlstm_kernel(x_ref, gt_ref, c0_ref, n0_ref, m0_ref, bi_ref, bf_ref, ng_ref,
                  o_ref, co_ref, no_ref, mo_ref, c_scr, n_scr, m_scr, *, nc):
    L = CHUNK
    rows = nc * L
    step = pl.program_id(1)

    @pl.when(step == 0)
    def _():
        c_scr[...] = c0_ref[0]
        n_scr[...] = n0_ref[0]
        m_scr[...] = m0_ref[0]

    gates = gt_ref[0]
    ig_t = gates + bi_ref[...]
    xf = gates + bf_ref[...]
    lf_t = jnp.minimum(xf, 0.0) - jnp.log(1.0 + jnp.exp(-jnp.abs(xf)))
    rr = lax.broadcasted_iota(jnp.int32, (rows, rows), 0)
    cc = lax.broadcasted_iota(jnp.int32, (rows, rows), 1)
    same_chunk = (rr // L) == (cc // L)
    bc_t = _dot_exact_lhs((cc <= rr) & same_chunk, lf_t)
    bc_tt = jnp.transpose(bc_t)
    ig_tt = jnp.transpose(ig_t)
    row = lax.broadcasted_iota(jnp.int32, (L, L), 0)
    col = lax.broadcasted_iota(jnp.int32, (L, L), 1)
    tri = col <= row
    lane = lax.broadcasted_iota(jnp.int32, (1, LANES), 1)
    m_all = m_scr[...]
    ng = ng_ref[...]

    inst = [(c, h) for c in range(nc) for h in range(H_C)]

    def sl(c, h, part):
        return x_ref[0, c * L:(c + 1) * L, part * D_C + h * DH_C:part * D_C + (h + 1) * DH_C]

    bcs = [bc_t[c * L:(c + 1) * L, G_F + h:G_F + h + 1] for c, h in inst]
    igs = [ig_t[c * L:(c + 1) * L, G_I + h:G_I + h + 1] for c, h in inst]
    dmats = [jnp.where(tri, bc - bc_tt[G_F + h:G_F + h + 1, c * L:(c + 1) * L]
                       + ig_tt[G_I + h:G_I + h + 1, c * L:(c + 1) * L], -jnp.inf)
             for (c, h), bc in zip(inst, bcs)]
    dmaxs = [jnp.max(d, axis=-1, keepdims=True) for d in dmats]
    ks = [sl(c, h, 1) * (DH_C ** -0.5) for c, h in inst]
    qk_raw = [_dot_nt(sl(c, h, 0), k) for (c, h), k in zip(inst, ks)]

    gs, ms = [None] * len(inst), [None] * len(inst)
    for h in range(H_C):
        m0 = m_all[:, h:h + 1]
        for c in range(nc):
            i = c * H_C + h
            gs[i] = bcs[i] + m0
            ms[i] = jnp.maximum(gs[i], dmaxs[i])
            m0 = ms[i][L - 1:L, :]
        m_all = jnp.where(lane == h, m0, m_all)
    m_last = [m[L - 1:L, :] for m in ms]
    decays = [jnp.exp(g[L - 1:L, :] - ml) for g, ml in zip(gs, m_last)]
    kds = [k * jnp.exp(bc[L - 1:L, :] - bc + ig - ml) for k, bc, ig, ml in zip(ks, bcs, igs, m_last)]
    kvs = [_dot(jnp.transpose(kd), sl(c, h, 2)) for (c, h), kd in zip(inst, kds)]
    ksums = [jnp.sum(kd, axis=0, keepdims=True) for kd in kds]

    cms, nrows = [None] * len(inst), [None] * len(inst)
    for h in range(H_C):
        cm, nrow = c_scr[h], n_scr[h:h + 1, :]
        for c in range(nc):
            i = c * H_C + h
            cms[i], nrows[i] = cm, nrow
            cm = decays[i] * cm + kvs[i]
            nrow = decays[i] * nrow + ksums[i]
        c_scr[h] = cm
        n_scr[h:h + 1, :] = nrow

    inters = [jnp.exp(g - m) for g, m in zip(gs, ms)]
    qks = [r * jnp.exp(d - m) for r, d, m in zip(qk_raw, dmats, ms)]
    nums = [it * _dot(sl(c, h, 0), cm) + _dot(qk, sl(c, h, 2))
            for (c, h), it, cm, qk in zip(inst, inters, cms, qks)]
    qns = [jnp.sum(sl(c, h, 0) * nrow, axis=-1, keepdims=True) for (c, h), nrow in zip(inst, nrows)]
    qksums = [jnp.sum(qk, axis=-1, keepdims=True) for qk in qks]
    dens = [it * qn + s for it, qn, s in zip(inters, qns, qksums)]
    hhs = [num / jnp.maximum(jnp.abs(den), jnp.exp(-m)) for num, den, m in zip(nums, dens, ms)]
    mss = [jnp.mean(hh * hh, axis=-1, keepdims=True) for hh in hhs]
    for (c, h), hh, msq in zip(inst, hhs, mss):
        o_ref[0, c * L:(c + 1) * L, h * DH_C:(h + 1) * DH_C] = (
            hh * lax.rsqrt(msq + EPS) * ng * _sigmoid(sl(c, h, 3)))
    m_scr[...] = m_all
    co_ref[0] = c_scr[...]
    no_ref[0] = n_scr[...]
    mo_ref[0] = m_all


def _mlstm(x, gates, c0, n0, m0, bi_row, bf_row, ng_row):
    b, t, _ = x.shape
    nc = min(MLSTM_CHUNKS, t // CHUNK)
    rows = nc * CHUNK
    blk = lambda bb, c: (bb, c, 0)
    per_b3 = lambda bb, c: (bb, 0, 0)
    per_b4 = lambda bb, c: (bb, 0, 0, 0)
    fixed = lambda bb, c: (0, 0)
    return pl.pallas_call(
        functools.partial(_mlstm_kernel, nc=nc),
        grid=(b, t // rows),
        in_specs=[pl.BlockSpec((1, rows, 4 * D_C), blk),
                  pl.BlockSpec((1, rows, LANES), blk),
                  pl.BlockSpec((1, H_C, DH_C, DH_C), per_b4),
                  pl.BlockSpec((1, H_C, DH_C), per_b3),
                  pl.BlockSpec((1, 1, LANES), per_b3),
                  pl.BlockSpec((1, LANES), fixed),
                  pl.BlockSpec((1, LANES), fixed),
                  pl.BlockSpec((1, DH_C), fixed)],
        out_specs=[pl.BlockSpec((1, rows, D_C), blk),
                   pl.BlockSpec((1, H_C, DH_C, DH_C), per_b4),
                   pl.BlockSpec((1, H_C, DH_C), per_b3),
                   pl.BlockSpec((1, 1, LANES), per_b3)],
        out_shape=[jax.ShapeDtypeStruct((b, t, D_C), F32),
                   jax.ShapeDtypeStruct((b, H_C, DH_C, DH_C), F32),
                   jax.ShapeDtypeStruct((b, H_C, DH_C), F32),
                   jax.ShapeDtypeStruct((b, 1, LANES), F32)],
        scratch_shapes=[pltpu.VMEM((H_C, DH_C, DH_C), F32),
                        pltpu.VMEM((H_C, DH_C), F32),
                        pltpu.VMEM((1, LANES), F32)],
        compiler_params=_params(("parallel", "arbitrary")),
        name="mlstm",
    )(x, gates, c0, n0, m0, bi_row, bf_row, ng_row)


def _mix_ffn_kernel(x_ref, a_ref, b_ref, c_ref, wo_ref, g0_ref, hist_ref, g1_ref, wu_ref, cw_ref,
                    wd_ref, g2_ref, o_ref, histo_ref, gp_scr, hist_scr, acc_scr, *, rows):
    t = pl.program_id(1)
    h0 = SUBLANES - (FFN_CONV - 1)

    @pl.when(t == 0)
    def _():
        hist_scr[h0:SUBLANES, :] = hist_ref[0]

    oa = jnp.concatenate([a_ref[hh] for hh in range(H_A)], axis=-1)
    mix = (_dot(oa, wo_ref[0:D_A, :])
           + _dot(b_ref[...], wo_ref[D_A:D_A + D_B, :])
           + _dot(c_ref[...], wo_ref[D_A + D_B:, :]))
    x = x_ref[0] + _rms(mix, g0_ref[...])
    h = _rms(x, g1_ref[...]).astype(BF16)
    acc_scr[...] = jnp.zeros_like(acc_scr)
    for ci in range(D_FF // FF_CHUNK):
        lo, hi = ci * FF_CHUNK, (ci + 1) * FF_CHUNK
        gate = jnp.dot(h, wu_ref[:, lo:hi], preferred_element_type=F32)
        up = jnp.dot(h, wu_ref[:, D_FF + lo:D_FF + hi], preferred_element_type=F32)
        gp_scr[h0:SUBLANES, :] = hist_scr[h0:SUBLANES, lo:hi]
        gp_scr[SUBLANES:SUBLANES + rows, :] = gate
        cw = cw_ref[:, lo:hi]
        conv = gate * cw[FFN_CONV - 1:FFN_CONV, :]
        for j in range(FFN_CONV - 1):
            conv = conv + gp_scr[h0 + j:h0 + j + rows, :] * cw[j:j + 1, :]
        hist_scr[h0:SUBLANES, lo:hi] = gp_scr[rows + h0:rows + SUBLANES, :]
        act = jax.nn.gelu(conv, approximate=True) * up
        acc_scr[...] += jnp.dot(act.astype(BF16), wd_ref[lo:hi, :], preferred_element_type=F32)
    histo_ref[0] = hist_scr[h0:SUBLANES, :]
    o_ref[0] = x + _rms(acc_scr[...], g2_ref[...])


def _mix_ffn(x, oa, ob, oc, w_out, g0, hist, g1, w_up, conv_w, w_down, g2):
    b, t, _ = x.shape
    r = min(ROW_TILE, t)
    assert t % r == 0
    nt = t // r
    blk = lambda bb, i: (bb, i, 0)
    flat = lambda bb, i: (bb * nt + i, 0)
    per_b = lambda bb, i: (bb, 0, 0)
    fixed = lambda bb, i: (0, 0)
    once = pl.Buffered(1)
    kern = functools.partial(_mix_ffn_kernel, rows=r)
    return pl.pallas_call(
        kern,
        grid=(b, nt),
        in_specs=[pl.BlockSpec((1, r, D_MODEL), blk),
                  pl.BlockSpec((H_A, r, DH_A), lambda bb, i: (0, bb * nt + i, 0)),
                  pl.BlockSpec((r, D_B), flat),
                  pl.BlockSpec((r, D_C), flat),
                  pl.BlockSpec(w_out.shape, fixed, pipeline_mode=once),
                  pl.BlockSpec((1, D_MODEL), fixed),
                  pl.BlockSpec((1, FFN_CONV - 1, D_FF), per_b),
                  pl.BlockSpec((1, D_MODEL), fixed),
                  pl.BlockSpec(w_up.shape, fixed, pipeline_mode=once),
                  pl.BlockSpec((FFN_CONV, D_FF), fixed),
                  pl.BlockSpec(w_down.shape, fixed, pipeline_mode=once),
                  pl.BlockSpec((1, D_MODEL), fixed)],
        out_specs=[pl.BlockSpec((1, r, D_MODEL), blk),
                   pl.BlockSpec((1, FFN_CONV - 1, D_FF), per_b)],
        out_shape=[jax.ShapeDtypeStruct((b, t, D_MODEL), F32),
                   jax.ShapeDtypeStruct((b, FFN_CONV - 1, D_FF), F32)],
        scratch_shapes=[pltpu.VMEM((SUBLANES + r, FF_CHUNK), F32),
                        pltpu.VMEM((SUBLANES, D_FF), F32),
                        pltpu.VMEM((r, D_MODEL), F32)],
        compiler_params=_params(("parallel", "arbitrary")),
        name="mix_ffn",
    )(x, oa, ob, oc, w_out, g0, hist, g1, w_up, conv_w, w_down, g2)


def _lane_row(vals, offset):
    return jnp.zeros((1, LANES), F32).at[0, offset:offset + vals.shape[0]].set(vals.astype(F32))


def _prep_weights(w_in, w_out, ffn_w_up, ffn_w_down):
    c_gate0 = 3 * D_A + 4 * D_B
    c_c0 = c_gate0 + 2 * H_B
    c_gate1 = c_c0 + 4 * D_C
    wb = w_in.astype(BF16)
    w_main = jnp.concatenate([wb[:, :c_gate0], wb[:, c_c0:c_gate1]], axis=1)
    n_gate = 2 * H_B + 2 * H_C
    w_gate = jnp.concatenate([wb[:, c_gate0:c_c0], wb[:, c_gate1:],
                              jnp.zeros((D_MODEL, LANES - n_gate), BF16)], axis=1)
    return w_main, w_gate, w_out.astype(BF16), ffn_w_up.astype(BF16), ffn_w_down.astype(BF16)


def _layer(x, kv_k, kv_v, gdn_hist, gdn_s, m_c, m_n, m_m, ffn_hist, lw):
    (g_mix_pre, g_mix_post, g_ffn_pre, g_ffn_post, w_main, w_gate, gdn_conv_w, alog_row, dtb_row,
     gdn_ng, bi_row, bf_row, mlstm_ng, w_out, w_up, ffn_conv_w, w_down) = lw
    b, t, _ = x.shape
    n = b * t
    xf = x.reshape(n, D_MODEL)
    q_hm, kt_hm, v_hm, k_f, v_f, qkv_b, z_b, qkvo_c, gates = _inproj(xf, g_mix_pre, w_main, w_gate)

    def streams_first(a):
        return a.reshape(H_A, b, t, DH_A).transpose(1, 0, 2, 3)

    k_new, v_new = streams_first(k_f), streams_first(v_f)
    g = b * H_A
    q_g = streams_first(q_hm).reshape(g, t, DH_A)
    if kv_k is None:
        pad = -(-t // ATTN_TK) * ATTN_TK - t
        kt_all = kt_hm.reshape(H_A, DH_A, b, t).transpose(2, 0, 1, 3)
        v_all = streams_first(v_hm)
        if pad > 0:
            kt_all = jnp.pad(kt_all, ((0, 0), (0, 0), (0, 0), (0, pad)))
            v_all = jnp.pad(v_all, ((0, 0), (0, 0), (0, pad), (0, 0)))
        o_a = _attention(q_g, kt_all.reshape(g, DH_A, t + pad), v_all.reshape(g, t + pad, DH_A))
    else:
        past = kv_k.shape[2]
        o_a = _attention_cached(q_g, kv_k.reshape(g, past, DH_A), kv_v.reshape(g, past, DH_A),
                                k_new.astype(BF16).reshape(g, t, DH_A),
                                streams_first(v_hm).reshape(g, t, DH_A))
    o_a = o_a.reshape(b, H_A, t, DH_A).transpose(1, 0, 2, 3).reshape(H_A, n, DH_A)

    o_b, gdn_hist_new, s_new = _gdn(qkv_b.reshape(b, t, 3 * D_B), z_b.reshape(b, t, D_B),
                                    gates.reshape(b, t, LANES), gdn_hist, gdn_s, gdn_conv_w,
                                    alog_row, dtb_row, gdn_ng)

    o_c, c_new, n_new, m_new = _mlstm(qkvo_c.reshape(b, t, 4 * D_C), gates.reshape(b, t, LANES),
                                      m_c, m_n, m_m, bi_row, bf_row, mlstm_ng)

    x2, ffn_hist_new = _mix_ffn(x, o_a, o_b.reshape(n, D_B), o_c.reshape(n, D_C), w_out, g_mix_post,
                                ffn_hist, g_ffn_pre, w_up, ffn_conv_w, w_down, g_ffn_post)
    return (x2, k_new, v_new, gdn_hist_new, s_new, c_new, n_new, m_new[:, 0, :H_C], ffn_hist_new)


def kernel(x_prompt, x_sample, cache_sb_k, cache_sb_v, state_gdn_conv, state_gdn_s, state_mlstm_c, state_mlstm_n, state_mlstm_m, state_ffn_conv, g_mix_pre, g_mix_post, g_ffn_pre, g_ffn_post, w_in, gdn_conv_w, gdn_a_log, gdn_dt_bias, gdn_norm_g, mlstm_b_i, mlstm_b_f, mlstm_norm_g, w_out, ffn_w_up, ffn_conv_w, ffn_w_down):
    depth = w_in.shape[0]
    bp = x_prompt.shape[0]
    bs = x_sample.shape[0]
    zero_gdn_hist = jnp.zeros((bp, GDN_CONV - 1, 3 * D_B), F32)
    zero_s = jnp.zeros((bp, H_B, DH_B, DH_B), F32)
    zero_c = jnp.zeros((bp, H_C, DH_C, DH_C), F32)
    zero_n = jnp.zeros((bp, H_C, DH_C), F32)
    m_init = jnp.full((bp, 1, LANES), NEG, F32)
    zero_ffn_hist = jnp.zeros((bp, FFN_CONV - 1, D_FF), F32)

    xp, xs = x_prompt, x_sample
    new_p, new_s = [], []
    for l in range(depth):
        w_main, w_gate, w_o, w_u, w_d = _prep_weights(w_in[l], w_out[l], ffn_w_up[l], ffn_w_down[l])
        lw = (g_mix_pre[l][None], g_mix_post[l][None], g_ffn_pre[l][None], g_ffn_post[l][None],
              w_main, w_gate, gdn_conv_w[l], _lane_row(gdn_a_log[l], G_A),
              _lane_row(gdn_dt_bias[l], G_A), gdn_norm_g[l][None],
              _lane_row(mlstm_b_i[l], G_I), _lane_row(mlstm_b_f[l], G_F), mlstm_norm_g[l][None],
              w_o, w_u, ffn_conv_w[l], w_d)
        xp, *st_p = _layer(xp, None, None, zero_gdn_hist, zero_s, zero_c, zero_n, m_init,
                           zero_ffn_hist, lw)
        m_s = jnp.zeros((bs, 1, LANES), F32).at[:, 0, :H_C].set(state_mlstm_m[l])
        xs, *st_s = _layer(xs, cache_sb_k[l], cache_sb_v[l], state_gdn_conv[l], state_gdn_s[l],
                           state_mlstm_c[l], state_mlstm_n[l], m_s, state_ffn_conv[l], lw)
        new_p.append(st_p)
        new_s.append(st_s)
    outs_p = [jnp.stack(a) for a in zip(*new_p)]
    outs_s = [jnp.stack(a) for a in zip(*new_s)]
    return (xp, xs, *outs_p, *outs_s)
```

```python
import functools

import jax
import jax.numpy as jnp
import numpy as np
from jax import lax
from jax.experimental import pallas as pl
from jax.experimental.pallas import tpu as pltpu

F32 = jnp.float32
BF16 = jnp.bfloat16

D_MODEL = 1024
CHUNK = 64
H_A, DH_A = 4, 64
H_B, DH_B = 4, 64
H_C, DH_C = 4, 128
D_A, D_B, D_C = H_A * DH_A, H_B * DH_B, H_C * DH_C
GDN_CONV = 4
D_FF = 2816
FFN_CONV = 3
EPS = 1e-6
NEG = -1e30

LANES = 128
SUBLANES = 8
ROW_TILE = 512
ATTN_TQ = 1024
ATTN_TK = 256
ATTN_UNROLL = 4
FF_CHUNK = 2816
GDN_CHUNKS = 4
MLSTM_CHUNKS = 4
VMEM_LIMIT = 56 * 1024 * 1024

G_BETA, G_A, G_I, G_F = 0, 4, 8, 12


def _params(sem):
    return pltpu.CompilerParams(dimension_semantics=sem, vmem_limit_bytes=VMEM_LIMIT)


def _dot(a, b):
    return jnp.dot(a.astype(BF16), b.astype(BF16), preferred_element_type=F32)


def _dot_nt(a, b):
    return lax.dot_general(a.astype(BF16), b.astype(BF16), (((1,), (1,)), ((), ())),
                           preferred_element_type=F32)


def _split(a):
    hi = a.astype(BF16)
    lo = (a - hi.astype(F32)).astype(BF16)
    return hi, lo


def _dot_exact_lhs(a01, b):
    a = a01.astype(BF16)
    b0 = b.astype(BF16)
    r1 = b - b0.astype(F32)
    b1 = r1.astype(BF16)
    b2 = (r1 - b1.astype(F32)).astype(BF16)
    return (jnp.dot(a, b0, preferred_element_type=F32)
            + jnp.dot(a, b1, preferred_element_type=F32)
            + jnp.dot(a, b2, preferred_element_type=F32))


def _sigmoid(x):
    return 1.0 / (1.0 + jnp.exp(-x))


def _softplus(x):
    return jnp.maximum(x, 0.0) + jnp.log(1.0 + jnp.exp(-jnp.abs(x)))


def _rms(x, g):
    return x * lax.rsqrt(jnp.mean(x * x, axis=-1, keepdims=True) + EPS) * g


def _inproj_kernel(x_ref, g_ref, w_ref, wg_ref, q_ref, kt_ref, v_ref, kf_ref, vf_ref,
                   b_ref, z_ref, c_ref, gt_ref):
    h = _rms(x_ref[...], g_ref[...]).astype(BF16)
    qkv = jnp.dot(h, w_ref[:, 0:3 * D_A], preferred_element_type=F32)
    for hh in range(H_A):
        lo = hh * DH_A
        q_ref[hh] = qkv[:, lo:lo + DH_A].astype(BF16)
        kf_ref[hh] = qkv[:, D_A + lo:D_A + lo + DH_A]
        v_h = qkv[:, 2 * D_A + lo:2 * D_A + lo + DH_A]
        vf_ref[hh] = v_h
        v_ref[hh] = v_h.astype(BF16)
    for pair in range(H_A * DH_A // LANES):
        kt_pair = jnp.transpose(qkv[:, D_A + pair * LANES:D_A + (pair + 1) * LANES])
        for sub in range(LANES // DH_A):
            kt_ref[pair * (LANES // DH_A) + sub] = kt_pair[sub * DH_A:(sub + 1) * DH_A, :].astype(BF16)
    b_ref[...] = jnp.dot(h, w_ref[:, 768:1536], preferred_element_type=F32)
    z_ref[...] = jnp.dot(h, w_ref[:, 1536:1792], preferred_element_type=F32)
    c_ref[...] = jnp.dot(h, w_ref[:, 1792:3840], preferred_element_type=F32)
    gt_ref[...] = jnp.dot(h, wg_ref[...], preferred_element_type=F32)


def _inproj(x, g, w_main, w_gate):
    n = x.shape[0]
    r = min(ROW_TILE, n)
    assert n % r == 0
    widths = (3 * D_B, D_B, 4 * D_C, LANES)
    row = lambda i: (i, 0)
    fixed = lambda i: (0, 0)
    head_rows = lambda i: (0, i, 0)
    head_cols = lambda i: (0, 0, i)
    hm_spec = pl.BlockSpec((H_A, r, DH_A), head_rows)
    hm = lambda dt: jax.ShapeDtypeStruct((H_A, n, DH_A), dt)
    return pl.pallas_call(
        _inproj_kernel,
        grid=(n // r,),
        in_specs=[pl.BlockSpec((r, D_MODEL), row),
                  pl.BlockSpec((1, D_MODEL), fixed),
                  pl.BlockSpec(w_main.shape, fixed),
                  pl.BlockSpec(w_gate.shape, fixed)],
        out_specs=[hm_spec, pl.BlockSpec((H_A, DH_A, r), head_cols), hm_spec, hm_spec, hm_spec]
                  + [pl.BlockSpec((r, w), row) for w in widths],
        out_shape=[hm(BF16), jax.ShapeDtypeStruct((H_A, DH_A, n), BF16), hm(BF16), hm(F32), hm(F32)]
                  + [jax.ShapeDtypeStruct((n, w), F32) for w in widths],
        compiler_params=_params(("parallel",)),
        name="inproj",
    )(x, g, w_main, w_gate)


def _attn_body(q_ref, u_ref, o_ref, qs_scr, z_scr, sp_scr, a_scr, acc_scr, car_scr,
               qk_block, v_block, n_blk, *, tq, tk, unroll, q_start):
    i = pl.program_id(1)
    q_pos0 = q_start + i * tq
    n_all = (q_pos0 + tq - 1 + tk - 1) // tk
    n_full = q_pos0 // tk
    qs_scr[...] = q_ref[0] * jnp.asarray(DH_A ** -0.5, BF16)
    acc_scr[...] = jnp.zeros_like(acc_scr)
    car_scr[...] = jnp.zeros_like(car_scr)
    z_scr[1] = jnp.full((tq, tk), NEG, BF16)
    sp_scr[1] = jnp.zeros((tq, tk), BF16)
    a_scr[0] = jnp.zeros((tq, tk), BF16)

    def stage_qk(j, masked, slot, r0=0):
        ks = pl.multiple_of(jnp.clip(j, 0, n_blk - 1) * tk, tk)
        z = qk_block(qs_scr[r0:, :], ks)
        if masked:
            kpos = j * tk + lax.broadcasted_iota(jnp.int32, (tq - r0, tk), 1)
            qpos = q_pos0 + r0 + lax.broadcasted_iota(jnp.int32, (tq - r0, tk), 0)
            z = jnp.where(kpos < qpos, z, NEG)
        zb = z.astype(BF16)
        z_scr[slot, r0:, :] = zb
        sp_scr[slot, r0:, :] = _softplus(zb)

    def stage_exp(slot, r0=0):
        mm = jnp.dot(sp_scr[slot, r0:, :], u_ref[...], preferred_element_type=F32)
        car = car_scr[r0:, :]
        rest = mm + jnp.concatenate([car] * (tk // LANES), axis=1)
        a_scr[slot, r0:, :] = jnp.exp(z_scr[slot, r0:, :] - rest.astype(BF16))
        car_scr[r0:, :] = car + jnp.broadcast_to(mm[:, 0:1], (tq - r0, LANES))

    def stage_pv(j, slot, r0=0):
        ks = pl.multiple_of(jnp.clip(j, 0, n_blk - 1) * tk, tk)
        acc_scr[r0:, :] += jnp.dot(a_scr[slot, r0:, :], v_block(ks), preferred_element_type=F32)

    def step(j, masked, slot):
        stage_qk(j, masked, slot)
        stage_exp(1 - slot)
        stage_pv(j + 2, slot)

    n_diag_static = tq // tk
    if q_start % (unroll * tk) == 0 and tq % tk == 0 and n_diag_static == unroll:
        n_full_u = n_full
        for u in range(unroll):
            d = unroll - 1 - u
            stage_qk(n_full + d, True, u % 2, d * tk)
            if d + 1 < unroll:
                stage_exp(1 - u % 2, (d + 1) * tk)
            if d + 2 < unroll:
                stage_pv(n_full + d + 2, u % 2, (d + 2) * tk)
    else:
        n_full_u = (n_full // unroll) * unroll
        n_diag_u = -(-(n_all - n_full_u) // unroll) * unroll

        def diag_body(t, c):
            j = n_full_u + n_diag_u - 1 - unroll * t
            for u in range(unroll):
                step(j - u, True, u % 2)
            return c

        lax.fori_loop(0, n_diag_u // unroll, diag_body, 0)

    def full_body(t, c):
        j = n_full_u - 1 - unroll * t
        for u in range(unroll):
            step(j - u, False, u % 2)
        return c

    lax.fori_loop(0, n_full_u // unroll, full_body, 0)
    stage_exp(1)
    stage_pv(1, 0)
    stage_pv(0, 1)
    o_ref[0] = acc_scr[...]


def _attn_kernel(q_ref, kt_ref, v_ref, u_ref, o_ref, *scratch, tk, **static):
    def qk_block(q, ks):
        return jnp.dot(q, kt_ref[0, :, pl.ds(ks, tk)], preferred_element_type=F32)

    def v_block(ks):
        return v_ref[0, pl.ds(ks, tk), :]

    _attn_body(q_ref, u_ref, o_ref, *scratch, qk_block, v_block, v_ref.shape[1] // tk, tk=tk, **static)


def _attn_cached_kernel(q_ref, kp_ref, vp_ref, kn_ref, vn_ref, u_ref, o_ref, *scratch, tk, **static):
    *pipe_scratch, k_scr, v_scr = scratch
    past, t = kp_ref.shape[1], kn_ref.shape[1]
    for dst, cached, new in ((k_scr, kp_ref, kn_ref), (v_scr, vp_ref, vn_ref)):
        dst[0:past, :] = cached[0].astype(BF16)
        dst[past:past + t, :] = new[0]
        dst[past + t:, :] = jnp.zeros((dst.shape[0] - past - t, dst.shape[1]), BF16)

    def qk_block(q, ks):
        return lax.dot_general(q, k_scr[pl.ds(ks, tk), :], (((1,), (1,)), ((), ())),
                               preferred_element_type=F32)

    def v_block(ks):
        return v_scr[pl.ds(ks, tk), :]

    _attn_body(q_ref, u_ref, o_ref, *pipe_scratch, qk_block, v_block, k_scr.shape[0] // tk, tk=tk,
               **static)


def _attn_call(kern, g, t, tq, tk, d, in_arrays, in_specs, extra_scratch):
    u = jnp.asarray((np.arange(tk)[:, None] >= np.arange(tk)[None, :]).astype(np.float32), BF16)
    return pl.pallas_call(
        kern,
        grid=(g, t // tq),
        in_specs=[pl.BlockSpec((1, tq, d), lambda h, i: (h, i, 0))] + in_specs
                 + [pl.BlockSpec(u.shape, lambda h, i: (0, 0))],
        out_specs=pl.BlockSpec((1, tq, d), lambda h, i: (h, i, 0)),
        out_shape=jax.ShapeDtypeStruct((g, t, d), F32),
        scratch_shapes=[pltpu.VMEM((tq, d), BF16), pltpu.VMEM((2, tq, tk), BF16),
                        pltpu.VMEM((2, tq, tk), BF16), pltpu.VMEM((2, tq, tk), BF16),
                        pltpu.VMEM((tq, d), F32), pltpu.VMEM((tq, LANES), F32)] + extra_scratch,
        compiler_params=_params(("parallel", "arbitrary")),
        name="attn",
    )(*in_arrays, u)


def _attn_static(t, q_start):
    tq, tk = min(ATTN_TQ, t), ATTN_TK
    assert t % tq == 0
    unroll = ATTN_UNROLL if tq >= ATTN_UNROLL * tk else 2
    return tq, tk, dict(tq=tq, tk=tk, unroll=unroll, q_start=q_start)


def _attention(q, kt, v):
    g, t, d = q.shape
    tk_total = v.shape[1]
    tq, tk, static = _attn_static(t, 0)
    whole = lambda h, i: (h, 0, 0)
    return _attn_call(functools.partial(_attn_kernel, **static), g, t, tq, tk, d, (q, kt, v),
                      [pl.BlockSpec((1, d, tk_total), whole), pl.BlockSpec((1, tk_total, d), whole)], [])


def _attention_cached(q, k_past, v_past, k_new, v_new):
    g, t, d = q.shape
    past = k_past.shape[1]
    tq, tk, static = _attn_static(t, past)
    tk_total = -(-(past + t) // tk) * tk
    whole = lambda h, i: (h, 0, 0)
    return _attn_call(functools.partial(_attn_cached_kernel, **static), g, t, tq, tk, d,
                      (q, k_past, v_past, k_new, v_new),
                      [pl.BlockSpec((1, past, d), whole), pl.BlockSpec((1, past, d), whole),
                       pl.BlockSpec((1, t, d), whole), pl.BlockSpec((1, t, d), whole)],
                      [pltpu.VMEM((tk_total, d), BF16), pltpu.VMEM((tk_total, d), BF16)])


def _gdn_kernel(x_ref, z_ref, gt_ref, hist_ref, s0_ref, cw_ref, alog_ref, dtb_ref, ng_ref,
                o_ref, histo_ref, so_ref, xp_scr, s_scr, *, nc):
    L = CHUNK
    rows = nc * L
    step = pl.program_id(1)
    h0 = SUBLANES - (GDN_CONV - 1)

    @pl.when(step == 0)
    def _():
        xp_scr[h0:SUBLANES, :] = hist_ref[0]
        s_scr[...] = s0_ref[0]

    x = x_ref[0]
    xp_scr[SUBLANES:SUBLANES + rows, :] = x
    cw = cw_ref[...]
    conv = x * cw[GDN_CONV - 1:GDN_CONV, :]
    xp = xp_scr[...]
    for j in range(GDN_CONV - 1):
        shifted = pltpu.roll(xp, SUBLANES + rows - (h0 + j), axis=0)[:rows, :]
        conv = conv + shifted * cw[j:j + 1, :]
    new_hist = xp_scr[rows + h0:rows + SUBLANES, :]
    xp_scr[h0:SUBLANES, :] = new_hist
    histo_ref[0] = new_hist
    act = conv * _sigmoid(conv)

    gates = gt_ref[0]
    beta_t = _sigmoid(gates)
    g_t = -jnp.exp(alog_ref[...]) * _softplus(gates + dtb_ref[...])
    rr = lax.broadcasted_iota(jnp.int32, (rows, rows), 0)
    cc = lax.broadcasted_iota(jnp.int32, (rows, rows), 1)
    same_chunk = (rr // L) == (cc // L)
    gc_t = _dot_exact_lhs((cc <= rr) & same_chunk, g_t)
    egc_t = jnp.exp(gc_t)
    gc_tt = jnp.transpose(gc_t)
    row = lax.broadcasted_iota(jnp.int32, (L, L), 0)
    col = lax.broadcasted_iota(jnp.int32, (L, L), 1)
    tri = col <= row
    strict = col < row
    eye = (col == row).astype(F32)
    z_all = z_ref[0]
    ng = ng_ref[...]

    inst = [(c, h) for c in range(nc) for h in range(H_B)]

    def part(c, h, p):
        return act[c * L:(c + 1) * L, p * D_B + h * DH_B:p * D_B + (h + 1) * DH_B]

    def col(tile, c, lane0, h):
        return tile[c * L:(c + 1) * L, lane0 + h:lane0 + h + 1]

    hr = lax.broadcasted_iota(jnp.int32, (D_B, D_B), 0) // DH_B
    hc = lax.broadcasted_iota(jnp.int32, (D_B, D_B), 1) // DH_B
    head_ones = (hr == hc).astype(BF16)

    def head_sumsq(x):
        hi, lo = _split(x * x)
        return (jnp.dot(hi, head_ones, preferred_element_type=F32)
                + jnp.dot(lo, head_ones, preferred_element_type=F32))

    q_all, k_all = act[:, 0:D_B], act[:, D_B:2 * D_B]
    qn_all = q_all * (lax.rsqrt(head_sumsq(q_all) + EPS) * (DH_B ** -0.5))
    kn_all = k_all * lax.rsqrt(head_sumsq(k_all) + EPS)
    qn = [qn_all[c * L:(c + 1) * L, h * DH_B:(h + 1) * DH_B] for c, h in inst]
    kn = [kn_all[c * L:(c + 1) * L, h * DH_B:(h + 1) * DH_B] for c, h in inst]
    betas = [col(beta_t, c, G_BETA, h) for c, h in inst]
    gcs = [col(gc_t, c, G_A, h) for c, h in inst]
    egcs = [col(egc_t, c, G_A, h) for c, h in inst]
    gams = [jnp.exp(jnp.where(tri, gc - gc_tt[G_A + h:G_A + h + 1, c * L:(c + 1) * L], -jnp.inf))
            for (c, h), gc in zip(inst, gcs)]
    gls = [gc[L - 1:L, :] for gc in gcs]
    kks = [_dot_nt(k, k) for k in kn]
    nn = [-jnp.where(strict, beta * kk * gam, 0.0) for beta, kk, gam in zip(betas, kks, gams)]
    x_inv = [eye + n for n in nn]
    pw = nn
    for _ in range(5):
        pw = [_dot(p, p) for p in pw]
        x_inv = [xi + _dot(xi, p) for xi, p in zip(x_inv, pw)]
    us = [_dot(xi, beta * part(c, h, 2)) for (c, h), xi, beta in zip(inst, x_inv, betas)]
    ws = [_dot(xi, (beta * egc) * k) for xi, beta, egc, k in zip(x_inv, betas, egcs, kn)]
    qks = [jnp.where(tri, _dot_nt(q, k) * gam, 0.0) for q, k, gam in zip(qn, kn, gams)]
    qes = [q * egc for q, egc in zip(qn, egcs)]
    kdts = [jnp.transpose(k * jnp.exp(gl - gc)) for k, gl, gc in zip(kn, gls, gcs)]
    egls = [jnp.exp(gl) for gl in gls]
    kdw = [_dot(kdt, w) for kdt, w in zip(kdts, ws)]
    drive = [_dot(kdt, u) for kdt, u in zip(kdts, us)]

    states = [None] * len(inst)
    state = [s_scr[h] for h in range(H_B)]
    for c in range(nc):
        for h in range(H_B):
            states[c * H_B + h] = state[h]
        state = [state[h] * egls[c * H_B + h] - _dot(kdw[c * H_B + h], state[h]) + drive[c * H_B + h]
                 for h in range(H_B)]
    v_new = [u - _dot(w, s) for u, w, s in zip(us, ws, states)]
    outs = [_dot(qe, s) + _dot(qk, vn) for qe, s, qk, vn in zip(qes, states, qks, v_new)]
    o_ms = [jnp.mean(o * o, axis=-1, keepdims=True) for o in outs]
    o_n = [o * lax.rsqrt(ms + EPS) * ng for o, ms in zip(outs, o_ms)]
    for c in range(nc):
        zc = z_all[c * L:(c + 1) * L, :]
        o_ref[0, c * L:(c + 1) * L, :] = (
            jnp.concatenate(o_n[c * H_B:(c + 1) * H_B], axis=-1) * (zc * _sigmoid(zc)))
    for h in range(H_B):
        s_scr[h] = state[h]
    so_ref[0] = s_scr[...]


def _gdn(qkv, z, gates, hist, s0, conv_w, alog_row, dtb_row, ng_row):
    b, t, _ = qkv.shape
    nc = min(GDN_CHUNKS, t // CHUNK)
    rows = nc * CHUNK
    blk = lambda bb, c: (bb, c, 0)
    per_b3 = lambda bb, c: (bb, 0, 0)
    per_b4 = lambda bb, c: (bb, 0, 0, 0)
    fixed = lambda bb, c: (0, 0)
    return pl.pallas_call(
        functools.partial(_gdn_kernel, nc=nc),
        grid=(b, t // rows),
        in_specs=[pl.BlockSpec((1, rows, 3 * D_B), blk),
                  pl.BlockSpec((1, rows, D_B), blk),
                  pl.BlockSpec((1, rows, LANES), blk),
                  pl.BlockSpec((1, GDN_CONV - 1, 3 * D_B), per_b3),
                  pl.BlockSpec((1, H_B, DH_B, DH_B), per_b4),
                  pl.BlockSpec((GDN_CONV, 3 * D_B), fixed),
                  pl.BlockSpec((1, LANES), fixed),
                  pl.BlockSpec((1, LANES), fixed),
                  pl.BlockSpec((1, DH_B), fixed)],
        out_specs=[pl.BlockSpec((1, rows, D_B), blk),
                   pl.BlockSpec((1, GDN_CONV - 1, 3 * D_B), per_b3),
                   pl.BlockSpec((1, H_B, DH_B, DH_B), per_b4)],
        out_shape=[jax.ShapeDtypeStruct((b, t, D_B), F32),
                   jax.ShapeDtypeStruct((b, GDN_CONV - 1, 3 * D_B), F32),
                   jax.ShapeDtypeStruct((b, H_B, DH_B, DH_B), F32)],
        scratch_shapes=[pltpu.VMEM((SUBLANES + rows, 3 * D_B), F32),
                        pltpu.VMEM((H_B, DH_B, DH_B), F32)],
        compiler_params=_params(("parallel", "arbitrary")),
        name="gdn",
    )(qkv, z, gates, hist, s0, conv_w, alog_row, dtb_row, ng_row)


def _mlstm_kernel(x_ref, gt_ref, c0_ref, n0_ref, m0_ref, bi_ref, bf_ref, ng_ref,
                  o_ref, co_ref, no_ref, mo_ref, c_scr, n_scr, m_scr, *, nc):
    L = CHUNK
    rows = nc * L
    step = pl.program_id(1)

    @pl.when(step == 0)
    def _():
        c_scr[...] = c0_ref[0]
        n_scr[...] = n0_ref[0]
        m_scr[...] = m0_ref[0]

    gates = gt_ref[0]
    ig_t = gates + bi_ref[...]
    xf = gates + bf_ref[...]
    lf_t = jnp.minimum(xf, 0.0) - jnp.log(1.0 + jnp.exp(-jnp.abs(xf)))
    rr = lax.broadcasted_iota(jnp.int32, (rows, rows), 0)
    cc = lax.broadcasted_iota(jnp.int32, (rows, rows), 1)
    same_chunk = (rr // L) == (cc // L)
    bc_t = _dot_exact_lhs((cc <= rr) & same_chunk, lf_t)
    bc_tt = jnp.transpose(bc_t)
    ig_tt = jnp.transpose(ig_t)
    row = lax.broadcasted_iota(jnp.int32, (L, L), 0)
    col = lax.broadcasted_iota(jnp.int32, (L, L), 1)
    tri = col <= row
    lane = lax.broadcasted_iota(jnp.int32, (1, LANES), 1)
    m_all = m_scr[...]
    ng = ng_ref[...]

    inst = [(c, h) for c in range(nc) for h in range(H_C)]

    def sl(c, h, part):
        return x_ref[0, c * L:(c + 1) * L, part * D_C + h * DH_C:part * D_C + (h + 1) * DH_C]

    bcs = [bc_t[c * L:(c + 1) * L, G_F + h:G_F + h + 1] for c, h in inst]
    igs = [ig_t[c * L:(c + 1) * L, G_I + h:G_I + h + 1] for c, h in inst]
    dmats = [jnp.where(tri, bc - bc_tt[G_F + h:G_F + h + 1, c * L:(c + 1) * L]
                       + ig_tt[G_I + h:G_I + h + 1, c * L:(c + 1) * L], -jnp.inf)
             for (c, h), bc in zip(inst, bcs)]
    dmaxs = [jnp.max(d, axis=-1, keepdims=True) for d in dmats]
    ks = [sl(c, h, 1) * (DH_C ** -0.5) for c, h in inst]
    qk_raw = [_dot_nt(sl(c, h, 0), k) for (c, h), k in zip(inst, ks)]

    gs, ms = [None] * len(inst), [None] * len(inst)
    for h in range(H_C):
        m0 = m_all[:, h:h + 1]
        for c in range(nc):
            i = c * H_C + h
            gs[i] = bcs[i] + m0
            ms[i] = jnp.maximum(gs[i], dmaxs[i])
            m0 = ms[i][L - 1:L, :]
        m_all = jnp.where(lane == h, m0, m_all)
    m_last = [m[L - 1:L, :] for m in ms]
    decays = [jnp.exp(g[L - 1:L, :] - ml) for g, ml in zip(gs, m_last)]
    kds = [k * jnp.exp(bc[L - 1:L, :] - bc + ig - ml) for k, bc, ig, ml in zip(ks, bcs, igs, m_last)]
    kvs = [_dot(jnp.transpose(kd), sl(c, h, 2)) for (c, h), kd in zip(inst, kds)]
    ksums = [jnp.sum(kd, axis=0, keepdims=True) for kd in kds]

    cms, nrows = [None] * len(inst), [None] * len(inst)
    for h in range(H_C):
        cm, nrow = c_scr[h], n_scr[h:h + 1, :]
        for c in range(nc):
            i = c * H_C + h
            cms[i], nrows[i] = cm, nrow
            cm = decays[i] * cm + kvs[i]
            nrow = decays[i] * nrow + ksums[i]
        c_scr[h] = cm
        n_scr[h:h + 1, :] = nrow

    inters = [jnp.exp(g - m) for g, m in zip(gs, ms)]
    qks = [r * jnp.exp(d - m) for r, d, m in zip(qk_raw, dmats, ms)]
    nums = [it * _dot(sl(c, h, 0), cm) + _dot(qk, sl(c, h, 2))
            for (c, h), it, cm, qk in zip(inst, inters, cms, qks)]
    qns = [jnp.sum(sl(c, h, 0) * nrow, axis=-1, keepdims=True) for (c, h), nrow in zip(inst, nrows)]
    qksums = [jnp.sum(qk, axis=-1, keepdims=True) for qk in qks]
    dens = [it * qn + s for it, qn, s in zip(inters, qns, qksums)]
    hhs = [num / jnp.maximum(jnp.abs(den), jnp.exp(-m)) for num, den, m in zip(nums, dens, ms)]
    mss = [jnp.mean(hh * hh, axis=-1, keepdims=True) for hh in hhs]
    for (c, h), hh, msq in zip(inst, hhs, mss):
        o_ref[0, c * L:(c + 1) * L, h * DH_C:(h + 1) * DH_C] = (
            hh * lax.rsqrt(msq + EPS) * ng * _sigmoid(sl(c, h, 3)))
    m_scr[...] = m_all
    co_ref[0] = c_scr[...]
    no_ref[0] = n_scr[...]
    mo_ref[0] = m_all


def _mlstm(x, gates, c0, n0, m0, bi_row, bf_row, ng_row):
    b, t, _ = x.shape
    nc = min(MLSTM_CHUNKS, t // CHUNK)
    rows = nc * CHUNK
    blk = lambda bb, c: (bb, c, 0)
    per_b3 = lambda bb, c: (bb, 0, 0)
    per_b4 = lambda bb, c: (bb, 0, 0, 0)
    fixed = lambda bb, c: (0, 0)
    return pl.pallas_call(
        functools.partial(_mlstm_kernel, nc=nc),
        grid=(b, t // rows),
        in_specs=[pl.BlockSpec((1, rows, 4 * D_C), blk),
                  pl.BlockSpec((1, rows, LANES), blk),
                  pl.BlockSpec((1, H_C, DH_C, DH_C), per_b4),
                  pl.BlockSpec((1, H_C, DH_C), per_b3),
                  pl.BlockSpec((1, 1, LANES), per_b3),
                  pl.BlockSpec((1, LANES), fixed),
                  pl.BlockSpec((1, LANES), fixed),
                  pl.BlockSpec((1, DH_C), fixed)],
        out_specs=[pl.BlockSpec((1, rows, D_C), blk),
                   pl.BlockSpec((1, H_C, DH_C, DH_C), per_b4),
                   pl.BlockSpec((1, H_C, DH_C), per_b3),
                   pl.BlockSpec((1, 1, LANES), per_b3)],
        out_shape=[jax.ShapeDtypeStruct((b, t, D_C), F32),
                   jax.ShapeDtypeStruct((b, H_C, DH_C, DH_C), F32),
                   jax.ShapeDtypeStruct((b, H_C, DH_C), F32),
                   jax.ShapeDtypeStruct((b, 1, LANES), F32)],
        scratch_shapes=[pltpu.VMEM((H_C, DH_C, DH_C), F32),
                        pltpu.VMEM((H_C, DH_C), F32),
                        pltpu.VMEM((1, LANES), F32)],
        compiler_params=_params(("parallel", "arbitrary")),
        name="mlstm",
    )(x, gates, c0, n0, m0, bi_row, bf_row, ng_row)


def _mix_ffn_kernel(x_ref, a_ref, b_ref, c_ref, wo_ref, g0_ref, hist_ref, g1_ref, wu_ref, cw_ref,
                    wd_ref, g2_ref, o_ref, histo_ref, gp_scr, hist_scr, acc_scr, *, rows):
    t = pl.program_id(1)
    h0 = SUBLANES - (FFN_CONV - 1)

    @pl.when(t == 0)
    def _():
        hist_scr[h0:SUBLANES, :] = hist_ref[0]

    oa = jnp.concatenate([a_ref[hh] for hh in range(H_A)], axis=-1)
    mix = (_dot(oa, wo_ref[0:D_A, :])
           + _dot(b_ref[...], wo_ref[D_A:D_A + D_B, :])
           + _dot(c_ref[...], wo_ref[D_A + D_B:, :]))
    x = x_ref[0] + _rms(mix, g0_ref[...])
    h = _rms(x, g1_ref[...]).astype(BF16)
    acc_scr[...] = jnp.zeros_like(acc_scr)
    for ci in range(D_FF // FF_CHUNK):
        lo, hi = ci * FF_CHUNK, (ci + 1) * FF_CHUNK
        gate = jnp.dot(h, wu_ref[:, lo:hi], preferred_element_type=F32)
        up = jnp.dot(h, wu_ref[:, D_FF + lo:D_FF + hi], preferred_element_type=F32)
        gp_scr[h0:SUBLANES, :] = hist_scr[h0:SUBLANES, lo:hi]
        gp_scr[SUBLANES:SUBLANES + rows, :] = gate
        cw = cw_ref[:, lo:hi]
        conv = gate * cw[FFN_CONV - 1:FFN_CONV, :]
        for j in range(FFN_CONV - 1):
            conv = conv + gp_scr[h0 + j:h0 + j + rows, :] * cw[j:j + 1, :]
        hist_scr[h0:SUBLANES, lo:hi] = gp_scr[rows + h0:rows + SUBLANES, :]
        act = jax.nn.gelu(conv, approximate=True) * up
        acc_scr[...] += jnp.dot(act.astype(BF16), wd_ref[lo:hi, :], preferred_element_type=F32)
    histo_ref[0] = hist_scr[h0:SUBLANES, :]
    o_ref[0] = x + _rms(acc_scr[...], g2_ref[...])


def _mix_ffn(x, oa, ob, oc, w_out, g0, hist, g1, w_up, conv_w, w_down, g2):
    b, t, _ = x.shape
    r = min(ROW_TILE, t)
    assert t % r == 0
    nt = t // r
    blk = lambda bb, i: (bb, i, 0)
    flat = lambda bb, i: (bb * nt + i, 0)
    per_b = lambda bb, i: (bb, 0, 0)
    fixed = lambda bb, i: (0, 0)
    once = pl.Buffered(1)
    kern = functools.partial(_mix_ffn_kernel, rows=r)
    return pl.pallas_call(
        kern,
        grid=(b, nt),
        in_specs=[pl.BlockSpec((1, r, D_MODEL), blk),
                  pl.BlockSpec((H_A, r, DH_A), lambda bb, i: (0, bb * nt + i, 0)),
                  pl.BlockSpec((r, D_B), flat),
                  pl.BlockSpec((r, D_C), flat),
                  pl.BlockSpec(w_out.shape, fixed, pipeline_mode=once),
                  pl.BlockSpec((1, D_MODEL), fixed),
                  pl.BlockSpec((1, FFN_CONV - 1, D_FF), per_b),
                  pl.BlockSpec((1, D_MODEL), fixed),
                  pl.BlockSpec(w_up.shape, fixed, pipeline_mode=once),
                  pl.BlockSpec((FFN_CONV, D_FF), fixed),
                  pl.BlockSpec(w_down.shape, fixed, pipeline_mode=once),
                  pl.BlockSpec((1, D_MODEL), fixed)],
        out_specs=[pl.BlockSpec((1, r, D_MODEL), blk),
                   pl.BlockSpec((1, FFN_CONV - 1, D_FF), per_b)],
        out_shape=[jax.ShapeDtypeStruct((b, t, D_MODEL), F32),
                   jax.ShapeDtypeStruct((b, FFN_CONV - 1, D_FF), F32)],
        scratch_shapes=[pltpu.VMEM((SUBLANES + r, FF_CHUNK), F32),
                        pltpu.VMEM((SUBLANES, D_FF), F32),
                        pltpu.VMEM((r, D_MODEL), F32)],
        compiler_params=_params(("parallel", "arbitrary")),
        name="mix_ffn",
    )(x, oa, ob, oc, w_out, g0, hist, g1, w_up, conv_w, w_down, g2)


def _lane_row(vals, offset):
    return jnp.zeros((1, LANES), F32).at[0, offset:offset + vals.shape[0]].set(vals.astype(F32))


def _prep_weights(w_in, w_out, ffn_w_up, ffn_w_down):
    c_gate0 = 3 * D_A + 4 * D_B
    c_c0 = c_gate0 + 2 * H_B
    c_gate1 = c_c0 + 4 * D_C
    wb = w_in.astype(BF16)
    w_main = jnp.concatenate([wb[:, :c_gate0], wb[:, c_c0:c_gate1]], axis=1)
    n_gate = 2 * H_B + 2 * H_C
    w_gate = jnp.concatenate([wb[:, c_gate0:c_c0], wb[:, c_gate1:],
                              jnp.zeros((D_MODEL, LANES - n_gate), BF16)], axis=1)
    return w_main, w_gate, w_out.astype(BF16), ffn_w_up.astype(BF16), ffn_w_down.astype(BF16)


def _layer(x, kv_k, kv_v, gdn_hist, gdn_s, m_c, m_n, m_m, ffn_hist, lw):
    (g_mix_pre, g_mix_post, g_ffn_pre, g_ffn_post, w_main, w_gate, gdn_conv_w, alog_row, dtb_row,
     gdn_ng, bi_row, bf_row, mlstm_ng, w_out, w_up, ffn_conv_w, w_down) = lw
    b, t, _ = x.shape
    n = b * t
    xf = x.reshape(n, D_MODEL)
    q_hm, kt_hm, v_hm, k_f, v_f, qkv_b, z_b, qkvo_c, gates = _inproj(xf, g_mix_pre, w_main, w_gate)

    def streams_first(a):
        return a.reshape(H_A, b, t, DH_A).transpose(1, 0, 2, 3)

    k_new, v_new = streams_first(k_f), streams_first(v_f)
    g = b * H_A
    q_g = streams_first(q_hm).reshape(g, t, DH_A)
    if kv_k is None:
        pad = -(-t // ATTN_TK) * ATTN_TK - t
        kt_all = kt_hm.reshape(H_A, DH_A, b, t).transpose(2, 0, 1, 3)
        v_all = streams_first(v_hm)
        if pad > 0:
            kt_all = jnp.pad(kt_all, ((0, 0), (0, 0), (0, 0), (0, pad)))
            v_all = jnp.pad(v_all, ((0, 0), (0, 0), (0, pad), (0, 0)))
        o_a = _attention(q_g, kt_all.reshape(g, DH_A, t + pad), v_all.reshape(g, t + pad, DH_A))
    else:
        past = kv_k.shape[2]
        o_a = _attention_cached(q_g, kv_k.reshape(g, past, DH_A), kv_v.reshape(g, past, DH_A),
                                k_new.astype(BF16).reshape(g, t, DH_A),
                                streams_first(v_hm).reshape(g, t, DH_A))
    o_a = o_a.reshape(b, H_A, t, DH_A).transpose(1, 0, 2, 3).reshape(H_A, n, DH_A)

    o_b, gdn_hist_new, s_new = _gdn(qkv_b.reshape(b, t, 3 * D_B), z_b.reshape(b, t, D_B),
                                    gates.reshape(b, t, LANES), gdn_hist, gdn_s, gdn_conv_w,
                                    alog_row, dtb_row, gdn_ng)

    o_c, c_new, n_new, m_new = _mlstm(qkvo_c.reshape(b, t, 4 * D_C), gates.reshape(b, t, LANES),
                                      m_c, m_n, m_m, bi_row, bf_row, mlstm_ng)

    x2, ffn_hist_new = _mix_ffn(x, o_a, o_b.reshape(n, D_B), o_c.reshape(n, D_C), w_out, g_mix_post,
                                ffn_hist, g_ffn_pre, w_up, ffn_conv_w, w_down, g_ffn_post)
    return (x2, k_new, v_new, gdn_hist_new, s_new, c_new, n_new, m_new[:, 0, :H_C], ffn_hist_new)


def kernel(x_prompt, x_sample, cache_sb_k, cache_sb_v, state_gdn_conv, state_gdn_s, state_mlstm_c, state_mlstm_n, state_mlstm_m, state_ffn_conv, g_mix_pre, g_mix_post, g_ffn_pre, g_ffn_post, w_in, gdn_conv_w, gdn_a_log, gdn_dt_bias, gdn_norm_g, mlstm_b_i, mlstm_b_f, mlstm_norm_g, w_out, ffn_w_up, ffn_conv_w, ffn_w_down):
    depth = w_in.shape[0]
    bp = x_prompt.shape[0]
    bs = x_sample.shape[0]
    zero_gdn_hist = jnp.zeros((bp, GDN_CONV - 1, 3 * D_B), F32)
    zero_s = jnp.zeros((bp, H_B, DH_B, DH_B), F32)
    zero_c = jnp.zeros((bp, H_C, DH_C, DH_C), F32)
    zero_n = jnp.zeros((bp, H_C, DH_C), F32)
    m_init = jnp.full((bp, 1, LANES), NEG, F32)
    zero_ffn_hist = jnp.zeros((bp, FFN_CONV - 1, D_FF), F32)

    xp, xs = x_prompt, x_sample
    new_p, new_s = [], []
    for l in range(depth):
        w_main, w_gate, w_o, w_u, w_d = _prep_weights(w_in[l], w_out[l], ffn_w_up[l], ffn_w_down[l])
        lw = (g_mix_pre[l][None], g_mix_post[l][None], g_ffn_pre[l][None], g_ffn_post[l][None],
              w_main, w_gate, gdn_conv_w[l], _lane_row(gdn_a_log[l], G_A),
              _lane_row(gdn_dt_bias[l], G_A), gdn_norm_g[l][None],
              _lane_row(mlstm_b_i[l], G_I), _lane_row(mlstm_b_f[l], G_F), mlstm_norm_g[l][None],
              w_o, w_u, ffn_conv_w[l], w_d)
        xp, *st_p = _layer(xp, None, None, zero_gdn_hist, zero_s, zero_c, zero_n, m_init,
                           zero_ffn_hist, lw)
        m_s = jnp.zeros((bs, 1, LANES), F32).at[:, 0, :H_C].set(state_mlstm_m[l])
        xs, *st_s = _layer(xs, cache_sb_k[l], cache_sb_v[l], state_gdn_conv[l], state_gdn_s[l],
                           state_mlstm_c[l], state_mlstm_n[l], m_s, state_ffn_conv[l], lw)
        new_p.append(st_p)
        new_s.append(st_s)
    outs_p = [jnp.stack(a) for a in zip(*new_p)]
    outs_s = [jnp.stack(a) for a in zip(*new_s)]
    return (xp, xs, *outs_p, *outs_s)
```

```python
import functools

import jax
import jax.numpy as jnp
import numpy as np
from jax import lax
from jax.experimental import pallas as pl
from jax.experimental.pallas import tpu as pltpu

F32 = jnp.float32
BF16 = jnp.bfloat16

D_MODEL = 1024
CHUNK = 64
H_A, DH_A = 4, 64
H_B, DH_B = 4, 64
H_C, DH_C = 4, 128
D_A, D_B, D_C = H_A * DH_A, H_B * DH_B, H_C * DH_C
GDN_CONV = 4
D_FF = 2816
FFN_CONV = 3
EPS = 1e-6
NEG = -1e30

LANES = 128
SUBLANES = 8
ROW_TILE = 512
ATTN_TQ = 1024
ATTN_TK = 256
ATTN_UNROLL = 4
FF_CHUNK = 2816
GDN_CHUNKS = 4
MLSTM_CHUNKS = 4
VMEM_LIMIT = 56 * 1024 * 1024

G_BETA, G_A, G_I, G_F = 0, 4, 8, 12


def _params(sem):
    return pltpu.CompilerParams(dimension_semantics=sem, vmem_limit_bytes=VMEM_LIMIT)


def _dot(a, b):
    return jnp.dot(a.astype(BF16), b.astype(BF16), preferred_element_type=F32)


def _dot_nt(a, b):
    return lax.dot_general(a.astype(BF16), b.astype(BF16), (((1,), (1,)), ((), ())),
                           preferred_element_type=F32)


def _split(a):
    hi = a.astype(BF16)
    lo = (a - hi.astype(F32)).astype(BF16)
    return hi, lo


def _dot_exact_lhs(a01, b):
    a = a01.astype(BF16)
    b0 = b.astype(BF16)
    r1 = b - b0.astype(F32)
    b1 = r1.astype(BF16)
    b2 = (r1 - b1.astype(F32)).astype(BF16)
    return (jnp.dot(a, b0, preferred_element_type=F32)
            + jnp.dot(a, b1, preferred_element_type=F32)
            + jnp.dot(a, b2, preferred_element_type=F32))


def _sigmoid(x):
    return 1.0 / (1.0 + jnp.exp(-x))


def _softplus(x):
    return jnp.maximum(x, 0.0) + jnp.log(1.0 + jnp.exp(-jnp.abs(x)))


def _rms(x, g):
    return x * lax.rsqrt(jnp.mean(x * x, axis=-1, keepdims=True) + EPS) * g


def _inproj_kernel(x_ref, g_ref, w_ref, wg_ref, q_ref, kt_ref, v_ref, kf_ref, vf_ref,
                   b_ref, z_ref, c_ref, gt_ref):
    h = _rms(x_ref[...], g_ref[...]).astype(BF16)
    qkv = jnp.dot(h, w_ref[:, 0:3 * D_A], preferred_element_type=F32)
    for hh in range(H_A):
        lo = hh * DH_A
        q_ref[hh] = qkv[:, lo:lo + DH_A].astype(BF16)
        kf_ref[hh] = qkv[:, D_A + lo:D_A + lo + DH_A]
        v_h = qkv[:, 2 * D_A + lo:2 * D_A + lo + DH_A]
        vf_ref[hh] = v_h
        v_ref[hh] = v_h.astype(BF16)
    for pair in range(H_A * DH_A // LANES):
        kt_pair = jnp.transpose(qkv[:, D_A + pair * LANES:D_A + (pair + 1) * LANES])
        for sub in range(LANES // DH_A):
            kt_ref[pair * (LANES // DH_A) + sub] = kt_pair[sub * DH_A:(sub + 1) * DH_A, :].astype(BF16)
    b_ref[...] = jnp.dot(h, w_ref[:, 768:1536], preferred_element_type=F32)
    z_ref[...] = jnp.dot(h, w_ref[:, 1536:1792], preferred_element_type=F32)
    c_ref[...] = jnp.dot(h, w_ref[:, 1792:3840], preferred_element_type=F32)
    gt_ref[...] = jnp.dot(h, wg_ref[...], preferred_element_type=F32)


def _inproj(x, g, w_main, w_gate):
    n = x.shape[0]
    r = min(ROW_TILE, n)
    assert n % r == 0
    widths = (3 * D_B, D_B, 4 * D_C, LANES)
    row = lambda i: (i, 0)
    fixed = lambda i: (0, 0)
    head_rows = lambda i: (0, i, 0)
    head_cols = lambda i: (0, 0, i)
    hm_spec = pl.BlockSpec((H_A, r, DH_A), head_rows)
    hm = lambda dt: jax.ShapeDtypeStruct((H_A, n, DH_A), dt)
    return pl.pallas_call(
        _inproj_kernel,
        grid=(n // r,),
        in_specs=[pl.BlockSpec((r, D_MODEL), row),
                  pl.BlockSpec((1, D_MODEL), fixed),
                  pl.BlockSpec(w_main.shape, fixed),
                  pl.BlockSpec(w_gate.shape, fixed)],
        out_specs=[hm_spec, pl.BlockSpec((H_A, DH_A, r), head_cols), hm_spec, hm_spec, hm_spec]
                  + [pl.BlockSpec((r, w), row) for w in widths],
        out_shape=[hm(BF16), jax.ShapeDtypeStruct((H_A, DH_A, n), BF16), hm(BF16), hm(F32), hm(F32)]
                  + [jax.ShapeDtypeStruct((n, w), F32) for w in widths],
        compiler_params=_params(("parallel",)),
        name="inproj",
    )(x, g, w_main, w_gate)


def _attn_body(q_ref, u_ref, o_ref, qs_scr, z_scr, sp_scr, a_scr, acc_scr, car_scr,
               qk_block, v_block, n_blk, *, tq, tk, unroll, q_start):
    i = pl.program_id(1)
    q_pos0 = q_start + i * tq
    n_all = (q_pos0 + tq - 1 + tk - 1) // tk
    n_full = q_pos0 // tk
    qs_scr[...] = q_ref[0] * jnp.asarray(DH_A ** -0.5, BF16)
    acc_scr[...] = jnp.zeros_like(acc_scr)
    car_scr[...] = jnp.zeros_like(car_scr)
    z_scr[1] = jnp.full((tq, tk), NEG, BF16)
    sp_scr[1] = jnp.zeros((tq, tk), BF16)
    a_scr[0] = jnp.zeros((tq, tk), BF16)

    def stage_qk(j, masked, slot, r0=0):
        ks = pl.multiple_of(jnp.clip(j, 0, n_blk - 1) * tk, tk)
        z = qk_block(qs_scr[r0:, :], ks)
        if masked:
            kpos = j * tk + lax.broadcasted_iota(jnp.int32, (tq - r0, tk), 1)
            qpos = q_pos0 + r0 + lax.broadcasted_iota(jnp.int32, (tq - r0, tk), 0)
            z = jnp.where(kpos < qpos, z, NEG)
        zb = z.astype(BF16)
        z_scr[slot, r0:, :] = zb
        sp_scr[slot, r0:, :] = _softplus(zb)

    def stage_exp(slot, r0=0):
        mm = jnp.dot(sp_scr[slot, r0:, :], u_ref[...], preferred_element_type=F32)
        car = car_scr[r0:, :]
        rest = mm + jnp.concatenate([car] * (tk // LANES), axis=1)
        a_scr[slot, r0:, :] = jnp.exp(z_scr[slot, r0:, :] - rest.astype(BF16))
        car_scr[r0:, :] = car + jnp.broadcast_to(mm[:, 0:1], (tq - r0, LANES))

    def stage_pv(j, slot, r0=0):
        ks = pl.multiple_of(jnp.clip(j, 0, n_blk - 1) * tk, tk)
        acc_scr[r0:, :] += jnp.dot(a_scr[slot, r0:, :], v_block(ks), preferred_element_type=F32)

    def step(j, masked, slot):
        stage_qk(j, masked, slot)
        stage_exp(1 - slot)
        stage_pv(j + 2, slot)

    n_diag_static = tq // tk
    if q_start % (unroll * tk) == 0 and tq % tk == 0 and n_diag_static == unroll:
        n_full_u = n_full
        for u in range(unroll):
            d = unroll - 1 - u
            stage_qk(n_full + d, True, u % 2, d * tk)
            if d + 1 < unroll:
                stage_exp(1 - u % 2, (d + 1) * tk)
            if d + 2 < unroll:
                stage_pv(n_full + d + 2, u % 2, (d + 2) * tk)
    else:
        n_full_u = (n_full // unroll) * unroll
        n_diag_u = -(-(n_all - n_full_u) // unroll) * unroll

        def diag_body(t, c):
            j = n_full_u + n_diag_u - 1 - unroll * t
            for u in range(unroll):
                step(j - u, True, u % 2)
            return c

        lax.fori_loop(0, n_diag_u // unroll, diag_body, 0)

    def full_body(t, c):
        j = n_full_u - 1 - unroll * t
        for u in range(unroll):
            step(j - u, False, u % 2)
        return c

    lax.fori_loop(0, n_full_u // unroll, full_body, 0)
    stage_exp(1)
    stage_pv(1, 0)
    stage_pv(0, 1)
    o_ref[0] = acc_scr[...]


def _attn_kernel(q_ref, kt_ref, v_ref, u_ref, o_ref, *scratch, tk, **static):
    def qk_block(q, ks):
        return jnp.dot(q, kt_ref[0, :, pl.ds(ks, tk)], preferred_element_type=F32)

    def v_block(ks):
        return v_ref[0, pl.ds(ks, tk), :]

    _attn_body(q_ref, u_ref, o_ref, *scratch, qk_block, v_block, v_ref.shape[1] // tk, tk=tk, **static)


def _attn_cached_kernel(q_ref, kp_ref, vp_ref, kn_ref, vn_ref, u_ref, o_ref, *scratch, tk, **static):
    *pipe_scratch, k_scr, v_scr = scratch
    past, t = kp_ref.shape[1], kn_ref.shape[1]
    for dst, cached, new in ((k_scr, kp_ref, kn_ref), (v_scr, vp_ref, vn_ref)):
        dst[0:past, :] = cached[0].astype(BF16)
        dst[past:past + t, :] = new[0]
        dst[past + t:, :] = jnp.zeros((dst.shape[0] - past - t, dst.shape[1]), BF16)

    def qk_block(q, ks):
        return lax.dot_general(q, k_scr[pl.ds(ks, tk), :], (((1,), (1,)), ((), ())),
                               preferred_element_type=F32)

    def v_block(ks):
        return v_scr[pl.ds(ks, tk), :]

    _attn_body(q_ref, u_ref, o_ref, *pipe_scratch, qk_block, v_block, k_scr.shape[0] // tk, tk=tk,
               **static)


def _attn_call(kern, g, t, tq, tk, d, in_arrays, in_specs, extra_scratch):
    u = jnp.asarray((np.arange(tk)[:, None] >= np.arange(tk)[None, :]).astype(np.float32), BF16)
    return pl.pallas_call(
        kern,
        grid=(g, t // tq),
        in_specs=[pl.BlockSpec((1, tq, d), lambda h, i: (h, i, 0))] + in_specs
                 + [pl.BlockSpec(u.shape, lambda h, i: (0, 0))],
        out_specs=pl.BlockSpec((1, tq, d), lambda h, i: (h, i, 0)),
        out_shape=jax.ShapeDtypeStruct((g, t, d), F32),
        scratch_shapes=[pltpu.VMEM((tq, d), BF16), pltpu.VMEM((2, tq, tk), BF16),
                        pltpu.VMEM((2, tq, tk), BF16), pltpu.VMEM((2, tq, tk), BF16),
                        pltpu.VMEM((tq, d), F32), pltpu.VMEM((tq, LANES), F32)] + extra_scratch,
        compiler_params=_params(("parallel", "arbitrary")),
        name="attn",
    )(*in_arrays, u)


def _attn_static(t, q_start):
    tq, tk = min(ATTN_TQ, t), ATTN_TK
    assert t % tq == 0
    unroll = ATTN_UNROLL if tq >= ATTN_UNROLL * tk else 2
    return tq, tk, dict(tq=tq, tk=tk, unroll=unroll, q_start=q_start)


def _attention(q, kt, v):
    g, t, d = q.shape
    tk_total = v.shape[1]
    tq, tk, static = _attn_static(t, 0)
    whole = lambda h, i: (h, 0, 0)
    return _attn_call(functools.partial(_attn_kernel, **static), g, t, tq, tk, d, (q, kt, v),
                      [pl.BlockSpec((1, d, tk_total), whole), pl.BlockSpec((1, tk_total, d), whole)], [])


def _attention_cached(q, k_past, v_past, k_new, v_new):
    g, t, d = q.shape
    past = k_past.shape[1]
    tq, tk, static = _attn_static(t, past)
    tk_total = -(-(past + t) // tk) * tk
    whole = lambda h, i: (h, 0, 0)
    return _attn_call(functools.partial(_attn_cached_kernel, **static), g, t, tq, tk, d,
                      (q, k_past, v_past, k_new, v_new),
                      [pl.BlockSpec((1, past, d), whole), pl.BlockSpec((1, past, d), whole),
                       pl.BlockSpec((1, t, d), whole), pl.BlockSpec((1, t, d), whole)],
                      [pltpu.VMEM((tk_total, d), BF16), pltpu.VMEM((tk_total, d), BF16)])


def _gdn_kernel(x_ref, z_ref, gt_ref, hist_ref, s0_ref, cw_ref, alog_ref, dtb_ref, ng_ref,
                o_ref, histo_ref, so_ref, xp_scr, s_scr, *, nc):
    L = CHUNK
    rows = nc * L
    step = pl.program_id(1)
    h0 = SUBLANES - (GDN_CONV - 1)

    @pl.when(step == 0)
    def _():
        xp_scr[h0:SUBLANES, :] = hist_ref[0]
        s_scr[...] = s0_ref[0]

    x = x_ref[0]
    xp_scr[SUBLANES:SUBLANES + rows, :] = x
    cw = cw_ref[...]
    conv = x * cw[GDN_CONV - 1:GDN_CONV, :]
    xp = xp_scr[...]
    for j in range(GDN_CONV - 1):
        shifted = pltpu.roll(xp, SUBLANES + rows - (h0 + j), axis=0)[:rows, :]
        conv = conv + shifted * cw[j:j + 1, :]
    new_hist = xp_scr[rows + h0:rows + SUBLANES, :]
    xp_scr[h0:SUBLANES, :] = new_hist
    histo_ref[0] = new_hist
    act = conv * _sigmoid(conv)

    gates = gt_ref[0]
    beta_t = _sigmoid(gates)
    g_t = -jnp.exp(alog_ref[...]) * _softplus(gates + dtb_ref[...])
    rr = lax.broadcasted_iota(jnp.int32, (rows, rows), 0)
    cc = lax.broadcasted_iota(jnp.int32, (rows, rows), 1)
    same_chunk = (rr // L) == (cc // L)
    gc_t = _dot_exact_lhs((cc <= rr) & same_chunk, g_t)
    egc_t = jnp.exp(gc_t)
    gc_tt = jnp.transpose(gc_t)
    row = lax.broadcasted_iota(jnp.int32, (L, L), 0)
    col = lax.broadcasted_iota(jnp.int32, (L, L), 1)
    tri = col <= row
    strict = col < row
    eye = (col == row).astype(F32)
    z_all = z_ref[0]
    ng = ng_ref[...]

    inst = [(c, h) for c in range(nc) for h in range(H_B)]

    def part(c, h, p):
        return act[c * L:(c + 1) * L, p * D_B + h * DH_B:p * D_B + (h + 1) * DH_B]

    def col(tile, c, lane0, h):
        return tile[c * L:(c + 1) * L, lane0 + h:lane0 + h + 1]

    hr = lax.broadcasted_iota(jnp.int32, (D_B, D_B), 0) // DH_B
    hc = lax.broadcasted_iota(jnp.int32, (D_B, D_B), 1) // DH_B
    head_ones = (hr == hc).astype(BF16)

    def head_sumsq(x):
        hi, lo = _split(x * x)
        return (jnp.dot(hi, head_ones, preferred_element_type=F32)
                + jnp.dot(lo, head_ones, preferred_element_type=F32))

    q_all, k_all = act[:, 0:D_B], act[:, D_B:2 * D_B]
    qn_all = q_all * (lax.rsqrt(head_sumsq(q_all) + EPS) * (DH_B ** -0.5))
    kn_all = k_all * lax.rsqrt(head_sumsq(k_all) + EPS)
    qn = [qn_all[c * L:(c + 1) * L, h * DH_B:(h + 1) * DH_B] for c, h in inst]
    kn = [kn_all[c * L:(c + 1) * L, h * DH_B:(h + 1) * DH_B] for c, h in inst]
    betas = [col(beta_t, c, G_BETA, h) for c, h in inst]
    gcs = [col(gc_t, c, G_A, h) for c, h in inst]
    egcs = [col(egc_t, c, G_A, h) for c, h in inst]
    gams = [jnp.exp(jnp.where(tri, gc - gc_tt[G_A + h:G_A + h + 1, c * L:(c + 1) * L], -jnp.inf))
            for (c, h), gc in zip(inst, gcs)]
    gls = [gc[L - 1:L, :] for gc in gcs]
    kks = [_dot_nt(k, k) for k in kn]
    nn = [-jnp.where(strict, beta * kk * gam, 0.0) for beta, kk, gam in zip(betas, kks, gams)]
    x_inv = [eye + n for n in nn]
    pw = nn
    for _ in range(5):
        pw = [_dot(p, p) for p in pw]
        x_inv = [xi + _dot(xi, p) for xi, p in zip(x_inv, pw)]
    us = [_dot(xi, beta * part(c, h, 2)) for (c, h), xi, beta in zip(inst, x_inv, betas)]
    ws = [_dot(xi, (beta * egc) * k) for xi, beta, egc, k in zip(x_inv, betas, egcs, kn)]
    qks = [jnp.where(tri, _dot_nt(q, k) * gam, 0.0) for q, k, gam in zip(qn, kn, gams)]
    qes = [q * egc for q, egc in zip(qn, egcs)]
    kdts = [jnp.transpose(k * jnp.exp(gl - gc)) for k, gl, gc in zip(kn, gls, gcs)]
    egls = [jnp.exp(gl) for gl in gls]
    kdw = [_dot(kdt, w) for kdt, w in zip(kdts, ws)]
    drive = [_dot(kdt, u) for kdt, u in zip(kdts, us)]

    states = [None] * len(inst)
    state = [s_scr[h] for h in range(H_B)]
    for c in range(nc):
        for h in range(H_B):
            states[c * H_B + h] = state[h]
        state = [state[h] * egls[c * H_B + h] - _dot(kdw[c * H_B + h], state[h]) + drive[c * H_B + h]
                 for h in range(H_B)]
    v_new = [u - _dot(w, s) for u, w, s in zip(us, ws, states)]
    outs = [_dot(qe, s) + _dot(qk, vn) for qe, s, qk, vn in zip(qes, states, qks, v_new)]
    o_ms = [jnp.mean(o * o, axis=-1, keepdims=True) for o in outs]
    o_n = [o * lax.rsqrt(ms + EPS) * ng for o, ms in zip(outs, o_ms)]
    for c in range(nc):
        zc = z_all[c * L:(c + 1) * L, :]
        o_ref[0, c * L:(c + 1) * L, :] = (
            jnp.concatenate(o_n[c * H_B:(c + 1) * H_B], axis=-1) * (zc * _sigmoid(zc)))
    for h in range(H_B):
        s_scr[h] = state[h]
    so_ref[0] = s_scr[...]


def _gdn(qkv, z, gates, hist, s0, conv_w, alog_row, dtb_row, ng_row):
    b, t, _ = qkv.shape
    nc = min(GDN_CHUNKS, t // CHUNK)
    rows = nc * CHUNK
    blk = lambda bb, c: (bb, c, 0)
    per_b3 = lambda bb, c: (bb, 0, 0)
    per_b4 = lambda bb, c: (bb, 0, 0, 0)
    fixed = lambda bb, c: (0, 0)
    return pl.pallas_call(
        functools.partial(_gdn_kernel, nc=nc),
        grid=(b, t // rows),
        in_specs=[pl.BlockSpec((1, rows, 3 * D_B), blk),
                  pl.BlockSpec((1, rows, D_B), blk),
                  pl.BlockSpec((1, rows, LANES), blk),
                  pl.BlockSpec((1, GDN_CONV - 1, 3 * D_B), per_b3),
                  pl.BlockSpec((1, H_B, DH_B, DH_B), per_b4),
                  pl.BlockSpec((GDN_CONV, 3 * D_B), fixed),
                  pl.BlockSpec((1, LANES), fixed),
                  pl.BlockSpec((1, LANES), fixed),
                  pl.BlockSpec((1, DH_B), fixed)],
        out_specs=[pl.BlockSpec((1, rows, D_B), blk),
                   pl.BlockSpec((1, GDN_CONV - 1, 3 * D_B), per_b3),
                   pl.BlockSpec((1, H_B, DH_B, DH_B), per_b4)],
        out_shape=[jax.ShapeDtypeStruct((b, t, D_B), F32),
                   jax.ShapeDtypeStruct((b, GDN_CONV - 1, 3 * D_B), F32),
                   jax.ShapeDtypeStruct((b, H_B, DH_B, DH_B), F32)],
        scratch_shapes=[pltpu.VMEM((SUBLANES + rows, 3 * D_B), F32),
                        pltpu.VMEM((H_B, DH_B, DH_B), F32)],
        compiler_params=_params(("parallel", "arbitrary")),
        name="gdn",
    )(qkv, z, gates, hist, s0, conv_w, alog_row, dtb_row, ng_row)


def _mlstm_kernel(x_ref, gt_ref, c0_ref, n0_ref, m0_ref, bi_ref, bf_ref, ng_ref,
                  o_ref, co_ref, no_ref, mo_ref, c_scr, n_scr, m_scr, *, nc):
    L = CHUNK
    rows = nc * L
    step = pl.program_id(1)

    @pl.when(step == 0)
    def _():
        c_scr[...] = c0_ref[0]
        n_scr[...] = n0_ref[0]
        m_scr[...] = m0_ref[0]

    gates = gt_ref[0]
    ig_t = gates + bi_ref[...]
    xf = gates + bf_ref[...]
    lf_t = jnp.minimum(xf, 0.0) - jnp.log(1.0 + jnp.exp(-jnp.abs(xf)))
    rr = lax.broadcasted_iota(jnp.int32, (rows, rows), 0)
    cc = lax.broadcasted_iota(jnp.int32, (rows, rows), 1)
    same_chunk = (rr // L) == (cc // L)
    bc_t = _dot_exact_lhs((cc <= rr) & same_chunk, lf_t)
    bc_tt = jnp.transpose(bc_t)
    ig_tt = jnp.transpose(ig_t)
    row = lax.broadcasted_iota(jnp.int32, (L, L), 0)
    col = lax.broadcasted_iota(jnp.int32, (L, L), 1)
    tri = col <= row
    lane = lax.broadcasted_iota(jnp.int32, (1, LANES), 1)
    m_all = m_scr[...]
    ng = ng_ref[...]

    inst = [(c, h) for c in range(nc) for h in range(H_C)]

    def sl(c, h, part):
        return x_ref[0, c * L:(c + 1) * L, part * D_C + h * DH_C:part * D_C + (h + 1) * DH_C]

    bcs = [bc_t[c * L:(c + 1) * L, G_F + h:G_F + h + 1] for c, h in inst]
    igs = [ig_t[c * L:(c + 1) * L, G_I + h:G_I + h + 1] for c, h in inst]
    dmats = [jnp.where(tri, bc - bc_tt[G_F + h:G_F + h + 1, c * L:(c + 1) * L]
                       + ig_tt[G_I + h:G_I + h + 1, c * L:(c + 1) * L], -jnp.inf)
             for (c, h), bc in zip(inst, bcs)]
    dmaxs = [jnp.max(d, axis=-1, keepdims=True) for d in dmats]
    ks = [sl(c, h, 1) * (DH_C ** -0.5) for c, h in inst]
    qk_raw = [_dot_nt(sl(c, h, 0), k) for (c, h), k in zip(inst, ks)]

    gs, ms = [None] * len(inst), [None] * len(inst)
    for h in range(H_C):
        m0 = m_all[:, h:h + 1]
        for c in range(nc):
            i = c * H_C + h
            gs[i] = bcs[i] + m0
            ms[i] = jnp.maximum(gs[i], dmaxs[i])
            m0 = ms[i][L - 1:L, :]
        m_all = jnp.where(lane == h, m0, m_all)
    m_last = [m[L - 1:L, :] for m in ms]
    decays = [jnp.exp(g[L - 1:L, :] - ml) for g, ml in zip(gs, m_last)]
    kds = [k * jnp.exp(bc[L - 1:L, :] - bc + ig - ml) for k, bc, ig, ml in zip(ks, bcs, igs, m_last)]
    kvs = [_dot(jnp.transpose(kd), sl(c, h, 2)) for (c, h), kd in zip(inst, kds)]
    ksums = [jnp.sum(kd, axis=0, keepdims=True) for kd in kds]

    cms, nrows = [None] * len(inst), [None] * len(inst)
    for h in range(H_C):
        cm, nrow = c_scr[h], n_scr[h:h + 1, :]
        for c in range(nc):
            i = c * H_C + h
            cms[i], nrows[i] = cm, nrow
            cm = decays[i] * cm + kvs[i]
            nrow = decays[i] * nrow + ksums[i]
        c_scr[h] = cm
        n_scr[h:h + 1, :] = nrow

    inters = [jnp.exp(g - m) for g, m in zip(gs, ms)]
    qks = [r * jnp.exp(d - m) for r, d, m in zip(qk_raw, dmats, ms)]
    ones_v = jnp.ones((L, DH_C), F32)
    qkv1 = [_dot(qk, jnp.concatenate([sl(c, h, 2), ones_v], axis=1)) for (c, h), qk in zip(inst, qks)]
    nums = [it * _dot(sl(c, h, 0), cm) + p[:, :DH_C]
            for (c, h), it, cm, p in zip(inst, inters, cms, qkv1)]
    qns = [jnp.sum(sl(c, h, 0) * nrow, axis=-1, keepdims=True) for (c, h), nrow in zip(inst, nrows)]
    qksums = [p[:, DH_C:DH_C + 1] for p in qkv1]
    dens = [it * qn + s for it, qn, s in zip(inters, qns, qksums)]
    hhs = [num / jnp.maximum(jnp.abs(den), jnp.exp(-m)) for num, den, m in zip(nums, dens, ms)]
    mss = [jnp.mean(hh * hh, axis=-1, keepdims=True) for hh in hhs]
    for (c, h), hh, msq in zip(inst, hhs, mss):
        o_ref[0, c * L:(c + 1) * L, h * DH_C:(h + 1) * DH_C] = (
            hh * lax.rsqrt(msq + EPS) * ng * _sigmoid(sl(c, h, 3)))
    m_scr[...] = m_all
    co_ref[0] = c_scr[...]
    no_ref[0] = n_scr[...]
    mo_ref[0] = m_all


def _mlstm(x, gates, c0, n0, m0, bi_row, bf_row, ng_row):
    b, t, _ = x.shape
    nc = min(MLSTM_CHUNKS, t // CHUNK)
    rows = nc * CHUNK
    blk = lambda bb, c: (bb, c, 0)
    per_b3 = lambda bb, c: (bb, 0, 0)
    per_b4 = lambda bb, c: (bb, 0, 0, 0)
    fixed = lambda bb, c: (0, 0)
    return pl.pallas_call(
        functools.partial(_mlstm_kernel, nc=nc),
        grid=(b, t // rows),
        in_specs=[pl.BlockSpec((1, rows, 4 * D_C), blk),
                  pl.BlockSpec((1, rows, LANES), blk),
                  pl.BlockSpec((1, H_C, DH_C, DH_C), per_b4),
                  pl.BlockSpec((1, H_C, DH_C), per_b3),
                  pl.BlockSpec((1, 1, LANES), per_b3),
                  pl.BlockSpec((1, LANES), fixed),
                  pl.BlockSpec((1, LANES), fixed),
                  pl.BlockSpec((1, DH_C), fixed)],
        out_specs=[pl.BlockSpec((1, rows, D_C), blk),
                   pl.BlockSpec((1, H_C, DH_C, DH_C), per_b4),
                   pl.BlockSpec((1, H_C, DH_C), per_b3),
                   pl.BlockSpec((1, 1, LANES), per_b3)],
        out_shape=[jax.ShapeDtypeStruct((b, t, D_C), F32),
                   jax.ShapeDtypeStruct((b, H_C, DH_C, DH_C), F32),
                   jax.ShapeDtypeStruct((b, H_C, DH_C), F32),
                   jax.ShapeDtypeStruct((b, 1, LANES), F32)],
        scratch_shapes=[pltpu.VMEM((H_C, DH_C, DH_C), F32),
                        pltpu.VMEM((H_C, DH_C), F32),
                        pltpu.VMEM((1, LANES), F32)],
        compiler_params=_params(("parallel", "arbitrary")),
        name="mlstm",
    )(x, gates, c0, n0, m0, bi_row, bf_row, ng_row)


def _mix_ffn_kernel(x_ref, a_ref, b_ref, c_ref, wo_ref, g0_ref, hist_ref, g1_ref, wu_ref, cw_ref,
                    wd_ref, g2_ref, o_ref, histo_ref, gp_scr, hist_scr, acc_scr, *, rows):
    t = pl.program_id(1)
    h0 = SUBLANES - (FFN_CONV - 1)

    @pl.when(t == 0)
    def _():
        hist_scr[h0:SUBLANES, :] = hist_ref[0]

    oa = jnp.concatenate([a_ref[hh] for hh in range(H_A)], axis=-1)
    mix = (_dot(oa, wo_ref[0:D_A, :])
           + _dot(b_ref[...], wo_ref[D_A:D_A + D_B, :])
           + _dot(c_ref[...], wo_ref[D_A + D_B:, :]))
    x = x_ref[0] + _rms(mix, g0_ref[...])
    h = _rms(x, g1_ref[...]).astype(BF16)
    acc_scr[...] = jnp.zeros_like(acc_scr)
    for ci in range(D_FF // FF_CHUNK):
        lo, hi = ci * FF_CHUNK, (ci + 1) * FF_CHUNK
        gate = jnp.dot(h, wu_ref[:, lo:hi], preferred_element_type=F32)
        up = jnp.dot(h, wu_ref[:, D_FF + lo:D_FF + hi], preferred_element_type=F32)
        gp_scr[h0:SUBLANES, :] = hist_scr[h0:SUBLANES, lo:hi]
        gp_scr[SUBLANES:SUBLANES + rows, :] = gate
        cw = cw_ref[:, lo:hi]
        conv = gate * cw[FFN_CONV - 1:FFN_CONV, :]
        for j in range(FFN_CONV - 1):
            conv = conv + gp_scr[h0 + j:h0 + j + rows, :] * cw[j:j + 1, :]
        hist_scr[h0:SUBLANES, lo:hi] = gp_scr[rows + h0:rows + SUBLANES, :]
        act = jax.nn.gelu(conv, approximate=True) * up
        acc_scr[...] += jnp.dot(act.astype(BF16), wd_ref[lo:hi, :], preferred_element_type=F32)
    histo_ref[0] = hist_scr[h0:SUBLANES, :]
    o_ref[0] = x + _rms(acc_scr[...], g2_ref[...])


def _mix_ffn(x, oa, ob, oc, w_out, g0, hist, g1, w_up, conv_w, w_down, g2):
    b, t, _ = x.shape
    r = min(ROW_TILE, t)
    assert t % r == 0
    nt = t // r
    blk = lambda bb, i: (bb, i, 0)
    flat = lambda bb, i: (bb * nt + i, 0)
    per_b = lambda bb, i: (bb, 0, 0)
    fixed = lambda bb, i: (0, 0)
    once = pl.Buffered(1)
    kern = functools.partial(_mix_ffn_kernel, rows=r)
    return pl.pallas_call(
        kern,
        grid=(b, nt),
        in_specs=[pl.BlockSpec((1, r, D_MODEL), blk),
                  pl.BlockSpec((H_A, r, DH_A), lambda bb, i: (0, bb * nt + i, 0)),
                  pl.BlockSpec((r, D_B), flat),
                  pl.BlockSpec((r, D_C), flat),
                  pl.BlockSpec(w_out.shape, fixed, pipeline_mode=once),
                  pl.BlockSpec((1, D_MODEL), fixed),
                  pl.BlockSpec((1, FFN_CONV - 1, D_FF), per_b),
                  pl.BlockSpec((1, D_MODEL), fixed),
                  pl.BlockSpec(w_up.shape, fixed, pipeline_mode=once),
                  pl.BlockSpec((FFN_CONV, D_FF), fixed),
                  pl.BlockSpec(w_down.shape, fixed, pipeline_mode=once),
                  pl.BlockSpec((1, D_MODEL), fixed)],
        out_specs=[pl.BlockSpec((1, r, D_MODEL), blk),
                   pl.BlockSpec((1, FFN_CONV - 1, D_FF), per_b)],
        out_shape=[jax.ShapeDtypeStruct((b, t, D_MODEL), F32),
                   jax.ShapeDtypeStruct((b, FFN_CONV - 1, D_FF), F32)],
        scratch_shapes=[pltpu.VMEM((SUBLANES + r, FF_CHUNK), F32),
                        pltpu.VMEM((SUBLANES, D_FF), F32),
                        pltpu.VMEM((r, D_MODEL), F32)],
        compiler_params=_params(("parallel", "arbitrary")),
        name="mix_ffn",
    )(x, oa, ob, oc, w_out, g0, hist, g1, w_up, conv_w, w_down, g2)


def _lane_row(vals, offset):
    return jnp.zeros((1, LANES), F32).at[0, offset:offset + vals.shape[0]].set(vals.astype(F32))


def _prep_weights(w_in, w_out, ffn_w_up, ffn_w_down):
    c_gate0 = 3 * D_A + 4 * D_B
    c_c0 = c_gate0 + 2 * H_B
    c_gate1 = c_c0 + 4 * D_C
    wb = w_in.astype(BF16)
    w_main = jnp.concatenate([wb[:, :c_gate0], wb[:, c_c0:c_gate1]], axis=1)
    n_gate = 2 * H_B + 2 * H_C
    w_gate = jnp.concatenate([wb[:, c_gate0:c_c0], wb[:, c_gate1:],
                              jnp.zeros((D_MODEL, LANES - n_gate), BF16)], axis=1)
    return w_main, w_gate, w_out.astype(BF16), ffn_w_up.astype(BF16), ffn_w_down.astype(BF16)


def _layer(x, kv_k, kv_v, gdn_hist, gdn_s, m_c, m_n, m_m, ffn_hist, lw):
    (g_mix_pre, g_mix_post, g_ffn_pre, g_ffn_post, w_main, w_gate, gdn_conv_w, alog_row, dtb_row,
     gdn_ng, bi_row, bf_row, mlstm_ng, w_out, w_up, ffn_conv_w, w_down) = lw
    b, t, _ = x.shape
    n = b * t
    xf = x.reshape(n, D_MODEL)
    q_hm, kt_hm, v_hm, k_f, v_f, qkv_b, z_b, qkvo_c, gates = _inproj(xf, g_mix_pre, w_main, w_gate)

    def streams_first(a):
        return a.reshape(H_A, b, t, DH_A).transpose(1, 0, 2, 3)

    k_new, v_new = streams_first(k_f), streams_first(v_f)
    g = b * H_A
    q_g = streams_first(q_hm).reshape(g, t, DH_A)
    if kv_k is None:
        pad = -(-t // ATTN_TK) * ATTN_TK - t
        kt_all = kt_hm.reshape(H_A, DH_A, b, t).transpose(2, 0, 1, 3)
        v_all = streams_first(v_hm)
        if pad > 0:
            kt_all = jnp.pad(kt_all, ((0, 0), (0, 0), (0, 0), (0, pad)))
            v_all = jnp.pad(v_all, ((0, 0), (0, 0), (0, pad), (0, 0)))
        o_a = _attention(q_g, kt_all.reshape(g, DH_A, t + pad), v_all.reshape(g, t + pad, DH_A))
    else:
        past = kv_k.shape[2]
        o_a = _attention_cached(q_g, kv_k.reshape(g, past, DH_A), kv_v.reshape(g, past, DH_A),
                                k_new.astype(BF16).reshape(g, t, DH_A),
                                streams_first(v_hm).reshape(g, t, DH_A))
    o_a = o_a.reshape(b, H_A, t, DH_A).transpose(1, 0, 2, 3).reshape(H_A, n, DH_A)

    o_b, gdn_hist_new, s_new = _gdn(qkv_b.reshape(b, t, 3 * D_B), z_b.reshape(b, t, D_B),
                                    gates.reshape(b, t, LANES), gdn_hist, gdn_s, gdn_conv_w,
                                    alog_row, dtb_row, gdn_ng)

    o_c, c_new, n_new, m_new = _mlstm(qkvo_c.reshape(b, t, 4 * D_C), gates.reshape(b, t, LANES),
                                      m_c, m_n, m_m, bi_row, bf_row, mlstm_ng)

    x2, ffn_hist_new = _mix_ffn(x, o_a, o_b.reshape(n, D_B), o_c.reshape(n, D_C), w_out, g_mix_post,
                                ffn_hist, g_ffn_pre, w_up, ffn_conv_w, w_down, g_ffn_post)
    return (x2, k_new, v_new, gdn_hist_new, s_new, c_new, n_new, m_new[:, 0, :H_C], ffn_hist_new)


def kernel(x_prompt, x_sample, cache_sb_k, cache_sb_v, state_gdn_conv, state_gdn_s, state_mlstm_c, state_mlstm_n, state_mlstm_m, state_ffn_conv, g_mix_pre, g_mix_post, g_ffn_pre, g_ffn_post, w_in, gdn_conv_w, gdn_a_log, gdn_dt_bias, gdn_norm_g, mlstm_b_i, mlstm_b_f, mlstm_norm_g, w_out, ffn_w_up, ffn_conv_w, ffn_w_down):
    depth = w_in.shape[0]
    bp = x_prompt.shape[0]
    bs = x_sample.shape[0]
    zero_gdn_hist = jnp.zeros((bp, GDN_CONV - 1, 3 * D_B), F32)
    zero_s = jnp.zeros((bp, H_B, DH_B, DH_B), F32)
    zero_c = jnp.zeros((bp, H_C, DH_C, DH_C), F32)
    zero_n = jnp.zeros((bp, H_C, DH_C), F32)
    m_init = jnp.full((bp, 1, LANES), NEG, F32)
    zero_ffn_hist = jnp.zeros((bp, FFN_CONV - 1, D_FF), F32)

    xp, xs = x_prompt, x_sample
    new_p, new_s = [], []
    for l in range(depth):
        w_main, w_gate, w_o, w_u, w_d = _prep_weights(w_in[l], w_out[l], ffn_w_up[l], ffn_w_down[l])
        lw = (g_mix_pre[l][None], g_mix_post[l][None], g_ffn_pre[l][None], g_ffn_post[l][None],
              w_main, w_gate, gdn_conv_w[l], _lane_row(gdn_a_log[l], G_A),
              _lane_row(gdn_dt_bias[l], G_A), gdn_norm_g[l][None],
              _lane_row(mlstm_b_i[l], G_I), _lane_row(mlstm_b_f[l], G_F), mlstm_norm_g[l][None],
              w_o, w_u, ffn_conv_w[l], w_d)
        xp, *st_p = _layer(xp, None, None, zero_gdn_hist, zero_s, zero_c, zero_n, m_init,
                           zero_ffn_hist, lw)
        m_s = jnp.zeros((bs, 1, LANES), F32).at[:, 0, :H_C].set(state_mlstm_m[l])
        xs, *st_s = _layer(xs, cache_sb_k[l], cache_sb_v[l], state_gdn_conv[l], state_gdn_s[l],
                           state_mlstm_c[l], state_mlstm_n[l], m_s, state_ffn_conv[l], lw)
        new_p.append(st_p)
        new_s.append(st_s)
    outs_p = [jnp.stack(a) for a in zip(*new_p)]
    outs_s = [jnp.stack(a) for a in zip(*new_s)]
    return (xp, xs, *outs_p, *outs_s)
```
